```python
import functools
import jax, jax.numpy as jnp
from jax import lax
import numpy as np

D_MODEL = 1024
BATCH = 2
SEQ = 8192
DEPTH = 4
DEC_BATCH = 128
DEC_SEQ = 1
PAST_LEN = 2048
PAGE_SIZE = 128

HEAD_DIM = 64
D_ATT = D_MODEL // 2
N_HEADS_ATT = D_ATT // HEAD_DIM
Q_BLOCK = 128
D_RG = D_MODEL // 4
RG_BLOCKS = 4
RG_BW = D_RG // RG_BLOCKS
RG_C = 8.0
CONV_W = 4
D_POOL = D_MODEL // 4
POOL_WINDOWS = (2, 4, 8, 16)
POOL_GW = D_POOL // len(POOL_WINDOWS)
POOL_BUF = max(POOL_WINDOWS) - 1
D_MIX = D_ATT + D_RG + D_POOL
OFF_Q = 0
OFF_K = OFF_Q + D_ATT
OFF_V = OFF_K + D_ATT
OFF_F = OFF_V + D_ATT
OFF_XB = OFF_F + N_HEADS_ATT
OFF_GB = OFF_XB + D_RG
OFF_U = OFF_GB + D_RG
N_IN = OFF_U + D_POOL
N_GROUPS = 4
EXPERTS_PER_GROUP = 8
N_EXPERTS = N_GROUPS * EXPERTS_PER_GROUP
TOP_K = 2
D_EXPERT = D_MODEL // 2
MOE_BLOCK = 128
EPS = 1e-6

kernel_name = 'hymba_fox_rglru_pool_hiermoe_step'


def rms_norm(x, g):
    xf = x.astype(jnp.float32)
    y = xf * lax.rsqrt(jnp.mean(xf * xf, axis=-1, keepdims=True) + EPS)
    return (y * g.astype(jnp.float32)).astype(x.dtype)


def fox_attend_prompt(q, k, v, logf):
    t, dh = q.shape[1], q.shape[-1]
    scale = dh ** -0.5
    c = jnp.cumsum(logf, axis=1).transpose(0, 2, 1)
    outs = []
    for s0 in range(0, t, Q_BLOCK):
        s1 = min(s0 + Q_BLOCK, t)
        logits = jnp.einsum('bqhd,bkhd->bhqk', q[:, s0:s1], k[:, :s1]).astype(jnp.float32) * scale
        logits = logits + c[:, :, s0:s1, None] - c[:, :, None, :s1]
        causal = jnp.arange(s0, s1)[:, None] >= jnp.arange(s1)[None, :]
        p = jax.nn.softmax(jnp.where(causal, logits, -jnp.inf), axis=-1)
        outs.append(jnp.einsum('bhqk,bkhd->bqhd', p.astype(v.dtype), v[:, :s1]))
    return jnp.concatenate(outs, axis=1)


def fox_attend_sample(q, k, v, logf, k_past, v_past, logf_past):
    p_len, t, dh = k_past.shape[1], q.shape[1], q.shape[-1]
    scale = dh ** -0.5
    c = jnp.cumsum(jnp.concatenate([logf_past.astype(jnp.float32), logf], axis=1), axis=1).transpose(0, 2, 1)
    cq = c[:, :, p_len:]
    s_past = jnp.einsum('bqhd,bkhd->bhqk', q, k_past).astype(jnp.float32) * scale + cq[..., None] - c[:, :, None, :p_len]
    s_new = jnp.einsum('bqhd,bkhd->bhqk', q, k).astype(jnp.float32) * scale + cq[..., None] - cq[:, :, None, :]
    causal = jnp.tril(jnp.ones((t, t), dtype=bool))
    s_new = jnp.where(causal, s_new, -jnp.inf)
    p = jax.nn.softmax(jnp.concatenate([s_past, s_new], axis=-1), axis=-1).astype(v.dtype)
    return (jnp.einsum('bhqk,bkhd->bqhd', p[..., :p_len], v_past)
            + jnp.einsum('bhqk,bkhd->bqhd', p[..., p_len:], v))


def _lin_combine(left, right):
    a_l, b_l = left
    a_r, b_r = right
    return a_l * a_r, a_r * b_l + b_r


def rglru_mixer(xb, gb, conv_past, h0, conv_w, conv_b, w_gate_a, b_gate_a, w_gate_x, b_gate_x, lru_lambda):
    b, t, c = xb.shape
    ext = jnp.concatenate([conv_past.astype(xb.dtype), xb], axis=1)
    xc = conv_b
    for j in range(CONV_W):
        xc = xc + ext[:, j:j + t] * conv_w[j]
    xf = xc.astype(jnp.float32)
    xh = xf.reshape(b, t, RG_BLOCKS, RG_BW)
    r = jax.nn.sigmoid(jnp.einsum('btnc,ncd->btnd', xh, w_gate_a.astype(jnp.float32)).reshape(b, t, c) + b_gate_a)
    i = jax.nn.sigmoid(jnp.einsum('btnc,ncd->btnd', xh, w_gate_x.astype(jnp.float32)).reshape(b, t, c) + b_gate_x)
    log_a = -RG_C * r * jax.nn.softplus(-lru_lambda.astype(jnp.float32))
    a = jnp.exp(log_a)
    u = jnp.sqrt(-jnp.expm1(2.0 * log_a)) * (i * xf)
    a_cum, h_part = lax.associative_scan(_lin_combine, (a, u), axis=1)
    h = h_part + a_cum * h0.astype(jnp.float32)[:, None, :]
    y = h * jax.nn.gelu(gb.astype(jnp.float32))
    return y.astype(xb.dtype), ext[:, -(CONV_W - 1):], h[:, -1].astype(xb.dtype)


def pool_mixer(u, u_past, pos0, w_pool, pool_scale):
    b, t, _ = u.shape
    ext = jnp.concatenate([u_past.astype(u.dtype), u], axis=1)
    p_len = ext.shape[1] - t
    n = p_len + t
    ef = ext.astype(jnp.float32)
    cs = jnp.concatenate([jnp.zeros((b, 1, D_POOL), jnp.float32), jnp.cumsum(ef, axis=1)], axis=1)
    pos = pos0 + jnp.arange(t)
    outs = []
    for g, w in enumerate(POOL_WINDOWS):
        sl = slice(g * POOL_GW, (g + 1) * POOL_GW)
        cpad = jnp.concatenate([jnp.zeros((b, w, POOL_GW), jnp.float32), cs[:, :, sl]], axis=1)
        win = cpad[:, p_len + 1 + w:n + 1 + w] - cpad[:, p_len + 1:n + 1]
        cnt = jnp.minimum(pos + 1, w).astype(jnp.float32)
        d = win / cnt[None, :, None] - ef[:, p_len:, sl]
        outs.append(jnp.einsum('btc,cd->btd', d, w_pool[g].astype(jnp.float32)))
    y = jnp.concatenate(outs, axis=-1) * pool_scale.astype(jnp.float32)
    return y.astype(u.dtype), ext[:, -POOL_BUF:]


def token_mix(h, attend, conv_past, h0, pool_past, pos0, w_in, b_forget, conv_w, conv_b, w_gate_a, b_gate_a,
              w_gate_x, b_gate_x, lru_lambda, w_pool, pool_scale, norm_groups, w_out):
    b, t, _ = h.shape
    proj = h @ w_in
    q = proj[..., OFF_Q:OFF_K].reshape(b, t, N_HEADS_ATT, HEAD_DIM)
    k = proj[..., OFF_K:OFF_V].reshape(b, t, N_HEADS_ATT, HEAD_DIM)
    v = proj[..., OFF_V:OFF_F].reshape(b, t, N_HEADS_ATT, HEAD_DIM)
    logf = jax.nn.log_sigmoid((proj[..., OFF_F:OFF_XB] + b_forget).astype(jnp.float32))
    att = attend(q, k, v, logf).reshape(b, t, D_ATT)
    rg, conv_new, h_last = rglru_mixer(proj[..., OFF_XB:OFF_GB], proj[..., OFF_GB:OFF_U], conv_past, h0,
                                       conv_w, conv_b, w_gate_a, b_gate_a, w_gate_x, b_gate_x, lru_lambda)
    pl, pool_new = pool_mixer(proj[..., OFF_U:N_IN], pool_past, pos0, w_pool, pool_scale)
    mixed = jnp.concatenate([rms_norm(att, norm_groups[:D_ATT]),
                             rms_norm(rg, norm_groups[D_ATT:D_ATT + D_RG]),
                             rms_norm(pl, norm_groups[D_ATT + D_RG:])], axis=-1)
    return mixed @ w_out, k, v, logf, conv_new, h_last, pool_new


def moe_ffn(h, w_router_group, b_router_group, w_router_expert, b_router_expert, w_up, w_gate, w_down):
    b, t, d = h.shape
    n = b * t
    xf = h.reshape(n, d)
    pg = jax.nn.softmax((xf @ w_router_group).astype(jnp.float32) + b_router_group.astype(jnp.float32), axis=-1)
    pg_top, grp = lax.top_k(pg, 1)
    le = ((xf @ w_router_expert).astype(jnp.float32) + b_router_expert.astype(jnp.float32)).reshape(n, N_GROUPS, EXPERTS_PER_GROUP)
    le = le[jnp.arange(n), grp[:, 0]]
    pe_top, e_local = lax.top_k(jax.nn.softmax(le, axis=-1), TOP_K)
    gate = pg_top * pe_top / jnp.sum(pe_top, axis=-1, keepdims=True)
    expert = grp * EXPERTS_PER_GROUP + e_local
    nk = n * TOP_K
    e_flat = expert.reshape(nk)
    tok_flat = jnp.repeat(jnp.arange(n), TOP_K)
    w_flat = gate.reshape(nk)
    order = jnp.argsort(e_flat)
    es, ts, ws = e_flat[order], tok_flat[order], w_flat[order]
    counts = jax.ops.segment_sum(jnp.ones((nk,), jnp.int32), e_flat, num_segments=N_EXPERTS)
    start = jnp.cumsum(counts) - counts
    pcounts = (counts + MOE_BLOCK - 1) // MOE_BLOCK * MOE_BLOCK
    pend = jnp.cumsum(pcounts)
    pstart = pend - pcounts
    dest = pstart[es] + jnp.arange(nk) - start[es]
    n_blocks = -(-(nk + N_EXPERTS * (MOE_BLOCK - 1)) // MOE_BLOCK)
    xbuf = jnp.zeros((n_blocks * MOE_BLOCK, d), xf.dtype).at[dest].set(xf[ts])
    blk_e = jnp.minimum(jnp.searchsorted(pend, jnp.arange(n_blocks) * MOE_BLOCK, side='right'), N_EXPERTS - 1)

    def expert_block(args):
        xb, e = args
        return (jax.nn.silu(xb @ w_gate[e]) * (xb @ w_up[e])) @ w_down[e]

    ybuf = lax.map(expert_block, (xbuf.reshape(n_blocks, MOE_BLOCK, d), blk_e)).reshape(n_blocks * MOE_BLOCK, d)
    out = jnp.zeros((n, d), ybuf.dtype).at[ts].add(ybuf[dest] * ws[:, None].astype(ybuf.dtype))
    return out.reshape(b, t, d).astype(h.dtype)


def setup_inputs(seed: int = 0) -> dict:
    key = jax.random.key(seed)
    ks = iter(jax.random.split(key, 48))

    def nrm(shape, s):
        return jax.random.normal(next(ks), shape, jnp.float32) * s

    n_pages = PAST_LEN // PAGE_SIZE
    n_pool_pages = (DEC_BATCH * n_pages * 5) // 4
    x_prompt = nrm((BATCH, SEQ, D_MODEL), 1.0)
    x_sample = nrm((DEC_BATCH, DEC_SEQ, D_MODEL), 1.0)
    cache_k = nrm((DEPTH, n_pool_pages, PAGE_SIZE, N_HEADS_ATT, HEAD_DIM), 1.0)
    cache_v = nrm((DEPTH, n_pool_pages, PAGE_SIZE, N_HEADS_ATT, HEAD_DIM), 1.0)
    cache_logf = jax.nn.log_sigmoid(3.5 + nrm((DEPTH, n_pool_pages, PAGE_SIZE, N_HEADS_ATT), 1.0))
    state_conv = nrm((DEPTH, DEC_BATCH, CONV_W - 1, D_RG), 1.0)
    state_rglru = nrm((DEPTH, DEC_BATCH, D_RG), 0.5)
    state_pool = nrm((DEPTH, DEC_BATCH, POOL_BUF, D_POOL), 1.0)
    page_table = jax.random.permutation(next(ks), n_pool_pages)[:DEC_BATCH * n_pages].reshape(DEC_BATCH, n_pages).astype(jnp.int32)
    norm_mix = 1.0 + nrm((DEPTH, D_MODEL), 0.02)
    w_in = nrm((DEPTH, D_MODEL, N_IN), D_MODEL ** -0.5)
    b_forget = jax.random.uniform(next(ks), (DEPTH, N_HEADS_ATT), jnp.float32, 1.0, 6.0)
    conv_w = nrm((DEPTH, CONV_W, D_RG), CONV_W ** -0.5)
    conv_b = nrm((DEPTH, D_RG), 0.02)
    w_gate_a = nrm((DEPTH, RG_BLOCKS, RG_BW, RG_BW), RG_BW ** -0.5)
    b_gate_a = nrm((DEPTH, D_RG), 0.02)
    w_gate_x = nrm((DEPTH, RG_BLOCKS, RG_BW, RG_BW), RG_BW ** -0.5)
    b_gate_x = nrm((DEPTH, D_RG), 0.02)
    a_c = jax.random.uniform(next(ks), (DEPTH, D_RG), jnp.float32, 0.9, 0.999)
    a_base = a_c ** (1.0 / RG_C)
    lru_lambda = jnp.log(a_base) - jnp.log1p(-a_base)
    w_pool = nrm((DEPTH, len(POOL_WINDOWS), POOL_GW, POOL_GW), POOL_GW ** -0.5)
    pool_scale = 1.0 + nrm((DEPTH, D_POOL), 0.1)
    norm_groups = 1.0 + nrm((DEPTH, D_MIX), 0.02)
    w_out = nrm((DEPTH, D_MIX, D_MODEL), D_MIX ** -0.5)
    norm_ffn = 1.0 + nrm((DEPTH, D_MODEL), 0.02)
    w_router_group = nrm((DEPTH, D_MODEL, N_GROUPS), D_MODEL ** -0.5)
    b_router_group = nrm((DEPTH, N_GROUPS), 0.01)
    w_router_expert = nrm((DEPTH, D_MODEL, N_EXPERTS), D_MODEL ** -0.5)
    b_router_expert = nrm((DEPTH, N_EXPERTS), 0.01)
    w_up = nrm((DEPTH, N_EXPERTS, D_MODEL, D_EXPERT), D_MODEL ** -0.5)
    w_gate = nrm((DEPTH, N_EXPERTS, D_MODEL, D_EXPERT), D_MODEL ** -0.5)
    w_down = nrm((DEPTH, N_EXPERTS, D_EXPERT, D_MODEL), D_EXPERT ** -0.5)
    norm_final = 1.0 + nrm((D_MODEL,), 0.02)
    return {'x_prompt': x_prompt, 'x_sample': x_sample, 'cache_k': cache_k, 'cache_v': cache_v,
            'cache_logf': cache_logf, 'state_conv': state_conv, 'state_rglru': state_rglru,
            'state_pool': state_pool, 'page_table': page_table, 'norm_mix': norm_mix, 'w_in': w_in,
            'b_forget': b_forget, 'conv_w': conv_w, 'conv_b': conv_b, 'w_gate_a': w_gate_a,
            'b_gate_a': b_gate_a, 'w_gate_x': w_gate_x, 'b_gate_x': b_gate_x, 'lru_lambda': lru_lambda,
            'w_pool': w_pool, 'pool_scale': pool_scale, 'norm_groups': norm_groups, 'w_out': w_out,
            'norm_ffn': norm_ffn, 'w_router_group': w_router_group, 'b_router_group': b_router_group,
            'w_router_expert': w_router_expert, 'b_router_expert': b_router_expert, 'w_up': w_up,
            'w_gate': w_gate, 'w_down': w_down, 'norm_final': norm_final}


def reference(x_prompt, x_sample, cache_k, cache_v, cache_logf, state_conv, state_rglru, state_pool, page_table,
              norm_mix, w_in, b_forget, conv_w, conv_b, w_gate_a, b_gate_a, w_gate_x, b_gate_x, lru_lambda,
              w_pool, pool_scale, norm_groups, w_out, norm_ffn, w_router_group, b_router_group,
              w_router_expert, b_router_expert, w_up, w_gate, w_down, norm_final):
    b_p = x_prompt.shape[0]
    b_s = x_sample.shape[0]
    past_len = page_table.shape[1] * cache_k.shape[2]
    xp, xs = x_prompt, x_sample
    kp_l, vp_l, fp_l, cp_l, hp_l, pp_l = [], [], [], [], [], []
    ks_l, vs_l, fs_l, cs_l, hs_l, ps_l = [], [], [], [], [], []
    for l in range(DEPTH):
        mix_w = (w_in[l], b_forget[l], conv_w[l], conv_b[l], w_gate_a[l], b_gate_a[l], w_gate_x[l], b_gate_x[l],
                 lru_lambda[l], w_pool[l], pool_scale[l], norm_groups[l], w_out[l])
        moe_w = (w_router_group[l], b_router_group[l], w_router_expert[l], b_router_expert[l],
                 w_up[l], w_gate[l], w_down[l])
        y, k, v, lf, cv, hl, pb = token_mix(
            rms_norm(xp, norm_mix[l]), fox_attend_prompt,
            jnp.zeros((b_p, CONV_W - 1, D_RG), xp.dtype), jnp.zeros((b_p, D_RG), jnp.float32),
            jnp.zeros((b_p, 0, D_POOL), xp.dtype), 0, *mix_w)
        xp = xp + y
        xp = xp + moe_ffn(rms_norm(xp, norm_ffn[l]), *moe_w)
        kp_l.append(k); vp_l.append(v); fp_l.append(lf); cp_l.append(cv); hp_l.append(hl); pp_l.append(pb)
        k_past = cache_k[l, page_table].reshape(b_s, past_len, N_HEADS_ATT, HEAD_DIM)
        v_past = cache_v[l, page_table].reshape(b_s, past_len, N_HEADS_ATT, HEAD_DIM)
        lf_past = cache_logf[l, page_table].reshape(b_s, past_len, N_HEADS_ATT)
        attend = functools.partial(fox_attend_sample, k_past=k_past, v_past=v_past, logf_past=lf_past)
        y, k, v, lf, cv, hl, pb = token_mix(
            rms_norm(xs, norm_mix[l]), attend, state_conv[l], state_rglru[l], state_pool[l], past_len, *mix_w)
        xs = xs + y
        xs = xs + moe_ffn(rms_norm(xs, norm_ffn[l]), *moe_w)
        ks_l.append(k); vs_l.append(v); fs_l.append(lf); cs_l.append(cv); hs_l.append(hl); ps_l.append(pb)
    y_prompt = rms_norm(xp, norm_final)
    y_sample = rms_norm(xs, norm_final)
    return (y_prompt, y_sample,
            jnp.stack(kp_l), jnp.stack(vp_l), jnp.stack(fp_l), jnp.stack(cp_l), jnp.stack(hp_l), jnp.stack(pp_l),
            jnp.stack(ks_l), jnp.stack(vs_l), jnp.stack(fs_l), jnp.stack(cs_l), jnp.stack(hs_l), jnp.stack(ps_l))
```

```python
import functools

import jax
import jax.numpy as jnp
from jax import lax
from jax.experimental import pallas as pl
from jax.experimental.pallas import tpu as pltpu

F32 = jnp.float32
BF16 = jnp.bfloat16

EPS = 1e-6
HEAD_DIM = 64
LANES = 128
RG_C = 8.0
RG_BLOCKS = 4
CONV_W = 4
POOL_WINDOWS = (2, 4, 8, 16)
POOL_BUF = max(POOL_WINDOWS) - 1
N_GROUPS = 4
EXPERTS_PER_GROUP = 8
N_EXPERTS = N_GROUPS * EXPERTS_PER_GROUP
TOP_K = 2
MOE_BLOCK = 256
NEG = -1e30
VMEM_LIMIT = 56 * 1024 * 1024


def _params(*sem):
    return pltpu.CompilerParams(dimension_semantics=sem, vmem_limit_bytes=VMEM_LIMIT)


def _split3(x):
    hi = x.astype(BF16)
    r1 = x - hi.astype(F32)
    mid = r1.astype(BF16)
    lo = (r1 - mid.astype(F32)).astype(BF16)
    return hi, mid, lo


def _dot(a, b):
    return jnp.dot(a, b, preferred_element_type=F32)


def _dot3(x, w_bf16):
    hi, mid, lo = _split3(x)
    return _dot(hi, w_bf16) + _dot(mid, w_bf16) + _dot(lo, w_bf16)


def _sigmoid(x):
    return 1.0 / (1.0 + jnp.exp(-x))


def _softplus(x):
    return jnp.maximum(x, 0.0) + jnp.log1p(jnp.exp(-jnp.abs(x)))


def _gelu_tanh(x):
    return 0.5 * x * (1.0 + jnp.tanh(0.7978845608028654 * (x + 0.044715 * (x * x * x))))


def _rms(x, g):
    return x * lax.rsqrt(jnp.mean(x * x, axis=-1, keepdims=True) + EPS) * g


def _inproj_kernel(x_ref, g_ref, w_ref, *outs, d_att, d_rest, head_layout, scale):
    xn = _rms(x_ref[...], g_ref[...]).astype(BF16)

    def sec(lo, n):
        return _dot(xn, w_ref[:, lo:lo + n])

    q = sec(0, d_att) * scale
    k = sec(d_att, d_att)
    v = sec(2 * d_att, d_att)
    rest = sec(3 * d_att, d_rest)
    f = sec(3 * d_att + d_rest, LANES)
    if not head_layout:
        q_ref, k_ref, v_ref, f_ref, r_ref = outs
        q_ref[...] = q
        k_ref[...] = k
        v_ref[...] = v
        f_ref[...] = f
        r_ref[...] = rest
        return
    qp_ref, kp_ref, vp_ref, k_ref, v_ref, f_ref, r_ref = outs
    k_ref[...] = k
    v_ref[...] = v
    f_ref[...] = f
    r_ref[...] = rest
    tm = q.shape[0]
    lane = lax.broadcasted_iota(jnp.int32, (tm, LANES), 1)
    low = lane < HEAD_DIM
    zero = jnp.zeros((tm, LANES), F32)
    ones_col = jnp.where(lane == HEAD_DIM, 1.0, 0.0)
    for val, ref, fill in ((q, qp_ref, zero), (k, kp_ref, zero), (v, vp_ref, ones_col)):
        for j in range(d_att // LANES):
            pair = val[:, j * LANES:(j + 1) * LANES]
            ref[2 * j] = jnp.where(low, pair, fill).astype(BF16)
            ref[2 * j + 1] = jnp.where(low, pltpu.roll(pair, HEAD_DIM, axis=1), fill).astype(BF16)


def _inproj(x, g, w, *, tm, head_layout):
    b, t, d = x.shape
    n_w = w.shape[1]
    d_att = 512
    d_rest = n_w - 3 * d_att - LANES
    n_heads = d_att // HEAD_DIM
    nt = t // tm
    row = lambda n: pl.BlockSpec((None, tm, n), lambda i, j: (i, j, 0))
    slab = pl.BlockSpec((None, n_heads, tm, LANES), lambda i, j: (i, 0, j, 0))
    f32s = lambda n: jax.ShapeDtypeStruct((b, t, n), F32)
    out_shape = [f32s(d_att), f32s(d_att), f32s(LANES), f32s(d_rest)]
    out_specs = [row(d_att), row(d_att), row(LANES), row(d_rest)]
    if head_layout:
        slab_s = jax.ShapeDtypeStruct((b, n_heads, t, LANES), BF16)
        out_shape = [slab_s, slab_s, slab_s] + out_shape
        out_specs = [slab, slab, slab] + out_specs
    else:
        out_shape = [f32s(d_att)] + out_shape
        out_specs = [row(d_att)] + out_specs
    kern = functools.partial(_inproj_kernel, d_att=d_att, d_rest=d_rest, head_layout=head_layout,
                             scale=HEAD_DIM ** -0.5)
    return pl.pallas_call(
        kern, grid=(b, nt), name="inproj_heads" if head_layout else "inproj_flat",
        in_specs=[row(d), pl.BlockSpec((1, d), lambda i, j: (0, 0)),
                  pl.BlockSpec((d, n_w), lambda i, j: (0, 0))],
        out_specs=out_specs, out_shape=out_shape,
        compiler_params=_params("parallel", "parallel"),
    )(x, g, w)


def _forget_kernel(f_ref, b_ref, tri_ref, lf_ref, nct_ref, carry_ref, *, n_heads):
    @pl.when(pl.program_id(1) == 0)
    def _():
        carry_ref[...] = jnp.zeros_like(carry_ref)

    lf = -_softplus(-(f_ref[...] + b_ref[...]))
    lf_ref[...] = lf
    cs = _dot3_left(tri_ref[...], lf) + carry_ref[...]
    carry_ref[...] = carry_ref[...] + jnp.sum(lf, axis=0, keepdims=True)
    nct_ref[...] = (-cs).T[0:n_heads, :]


def _dot3_left(w_bf16, x):
    hi, mid, lo = _split3(x)
    return _dot(w_bf16, hi) + _dot(w_bf16, mid) + _dot(w_bf16, lo)


def _forget(f_raw, b_pad, *, tc, n_heads):
    b, t, _ = f_raw.shape
    tri = (jnp.arange(tc)[:, None] >= jnp.arange(tc)[None, :]).astype(BF16)
    return pl.pallas_call(
        functools.partial(_forget_kernel, n_heads=n_heads), grid=(b, t // tc), name="forget",
        in_specs=[pl.BlockSpec((None, tc, LANES), lambda i, j: (i, j, 0)),
                  pl.BlockSpec((1, LANES), lambda i, j: (0, 0)),
                  pl.BlockSpec((tc, tc), lambda i, j: (0, 0))],
        out_specs=[pl.BlockSpec((None, tc, LANES), lambda i, j: (i, j, 0)),
                   pl.BlockSpec((None, n_heads, tc), lambda i, j: (i, 0, j))],
        out_shape=[jax.ShapeDtypeStruct((b, t, LANES), F32),
                   jax.ShapeDtypeStruct((b, n_heads, t), F32)],
        scratch_shapes=[pltpu.VMEM((1, LANES), F32)],
        compiler_params=_params("parallel", "arbitrary"),
    )(f_raw, b_pad, tri)


def _attention_kernel(q_ref, k_ref, v_ref, nc_ref, o_ref, *, tq):
    qi = pl.program_id(2)
    q = q_ref[...]

    def step(j, carry, masked):
        m, acc = carry
        off = pl.multiple_of(j * tq, tq)
        k = k_ref[pl.ds(off, tq), :]
        v = v_ref[pl.ds(off, tq), :]
        s = lax.dot_general(q, k, (((1,), (1,)), ((), ())), preferred_element_type=F32)
        s = s + nc_ref[:, pl.ds(off, tq)]
        if masked:
            row = lax.broadcasted_iota(jnp.int32, (tq, tq), 0)
            col = lax.broadcasted_iota(jnp.int32, (tq, tq), 1)
            s = jnp.where(row >= col, s, NEG)
        m_new = jnp.maximum(m, jnp.max(s, axis=-1, keepdims=True))
        p = jnp.exp(s - m_new)
        acc = jnp.exp(m - m_new) * acc + _dot(p.astype(BF16), v)
        return m_new, acc

    init = (jnp.full((tq, 1), NEG, F32), jnp.zeros((tq, LANES), F32))
    carry = lax.fori_loop(0, qi, lambda j, c: step(j, c, False), init)
    _, acc = step(qi, carry, True)
    o_ref[...] = acc / acc[:, HEAD_DIM:HEAD_DIM + 1]


def _attention(qp, kp, vp, nct, *, tq):
    b, h, t, _ = qp.shape
    nc4 = nct.reshape(b, h, 1, t)
    tile = pl.BlockSpec((None, None, tq, LANES), lambda i, j, n: (i, j, n, 0))
    full = pl.BlockSpec((None, None, t, LANES), lambda i, j, n: (i, j, 0, 0))
    return pl.pallas_call(
        functools.partial(_attention_kernel, tq=tq), grid=(b, h, t // tq), name="attention",
        in_specs=[tile, full, full, pl.BlockSpec((None, None, 1, t), lambda i, j, n: (i, j, 0, 0))],
        out_specs=tile, out_shape=jax.ShapeDtypeStruct((b, h, t, LANES), F32),
        compiler_params=_params("parallel", "parallel", "arbitrary"),
    )(qp, kp, vp, nc4)


def _rglru_gates(xc, gb, wa_ref, ba_ref, wx_ref, bx_ref, lam_ref):
    xcb = xc.astype(BF16)
    r = _sigmoid(_dot(xcb, wa_ref[...]) + ba_ref[...])
    i = _sigmoid(_dot(xcb, wx_ref[...]) + bx_ref[...])
    log_a = -RG_C * r * _softplus(-lam_ref[...])
    a = jnp.exp(log_a)
    u = jnp.sqrt(jnp.tanh(-log_a) * (a * a + 1.0)) * (i * xc)
    return a, u, _gelu_tanh(gb)


def _pool_lane_select(c, w2, w4, w8, w16):
    lane = lax.broadcasted_iota(jnp.int32, c.shape, 1)
    gw = c.shape[1] // len(POOL_WINDOWS)
    return jnp.where(lane < gw, w2, jnp.where(lane < 2 * gw, w4, jnp.where(lane < 3 * gw, w8, w16)))


def _mix_prompt_kernel(r_ref, cw_ref, cb_ref, wa_ref, ba_ref, wx_ref, bx_ref, lam_ref, wp_ref, ps_ref,
                       y_ref, conv_ref, h_ref, pool_ref, xprev_ref, uprev_ref, htail_ref, *, tr, c):
    t = pl.program_id(1)

    @pl.when(t == 0)
    def _():
        xprev_ref[...] = jnp.zeros_like(xprev_ref)
        uprev_ref[...] = jnp.zeros_like(uprev_ref)
        htail_ref[...] = jnp.zeros_like(htail_ref)

    xb = r_ref[:, 0:c]
    gb = r_ref[:, c:2 * c]
    u_in = r_ref[:, 2 * c:3 * c]

    ext = jnp.concatenate([xprev_ref[...], xb], axis=0)
    xc = cb_ref[...] + cw_ref[CONV_W - 1:CONV_W, :] * xb
    for j in range(1, CONV_W):
        xc = xc + cw_ref[CONV_W - 1 - j:CONV_W - j, :] * pltpu.roll(ext, j, axis=0)[8:]

    a, u, gate = _rglru_gates(xc, gb, wa_ref, ba_ref, wx_ref, bx_ref, lam_ref)

    row = lax.broadcasted_iota(jnp.int32, (tr, c), 0)
    big_a, big_u = a, u
    sh = 1
    while sh < tr:
        keep = row >= sh
        a_s = jnp.where(keep, pltpu.roll(big_a, sh, axis=0), 1.0)
        u_s = jnp.where(keep, pltpu.roll(big_u, sh, axis=0), 0.0)
        big_u = big_a * u_s + big_u
        big_a = big_a * a_s
        sh *= 2
    h = big_u + big_a * htail_ref[7:8, :]

    extu = jnp.concatenate([uprev_ref[...], u_in], axis=0)
    s2 = extu + pltpu.roll(extu, 1, axis=0)
    s4 = s2 + pltpu.roll(s2, 2, axis=0)
    s8 = s4 + pltpu.roll(s4, 4, axis=0)
    s16 = s8 + pltpu.roll(s8, 8, axis=0)
    win = _pool_lane_select(u_in, s2[16:], s4[16:], s8[16:], s16[16:])
    wsz = _pool_lane_select(u_in, 2.0, 4.0, 8.0, 16.0)
    pos = (t * tr + row).astype(F32)
    d = win / jnp.minimum(pos + 1.0, wsz) - u_in
    y_pool = _dot(d.astype(BF16), wp_ref[...]) * ps_ref[...]

    y_ref[:, 0:c] = h * gate
    y_ref[:, c:2 * c] = y_pool
    conv_ref[...] = xb[tr - 8:]
    h_ref[...] = h[tr - 8:]
    pool_ref[...] = u_in[tr - 16:]
    xprev_ref[...] = xb[tr - 8:]
    uprev_ref[...] = u_in[tr - 16:]
    htail_ref[...] = h[tr - 8:]


def _mix_prompt(rest, mw, *, tr):
    b, t, c3 = rest.shape
    c = c3 // 3
    vec = lambda n: pl.BlockSpec((n, c), lambda i, j: (0, 0))
    mat = pl.BlockSpec((c, c), lambda i, j: (0, 0))
    tail = lambda n: pl.BlockSpec((None, n, c), lambda i, j: (i, 0, 0))
    return pl.pallas_call(
        functools.partial(_mix_prompt_kernel, tr=tr, c=c), grid=(b, t // tr), name="mix_prompt",
        in_specs=[pl.BlockSpec((None, tr, c3), lambda i, j: (i, j, 0)),
                  vec(CONV_W), vec(1), mat, vec(1), mat, vec(1), vec(1), mat, vec(1)],
        out_specs=[pl.BlockSpec((None, tr, 2 * c), lambda i, j: (i, j, 0)), tail(8), tail(8), tail(16)],
        out_shape=[jax.ShapeDtypeStruct((b, t, 2 * c), F32), jax.ShapeDtypeStruct((b, 8, c), F32),
                   jax.ShapeDtypeStruct((b, 8, c), F32), jax.ShapeDtypeStruct((b, 16, c), F32)],
        scratch_shapes=[pltpu.VMEM((8, c), F32), pltpu.VMEM((16, c), F32), pltpu.VMEM((8, c), F32)],
        compiler_params=_params("parallel", "arbitrary"),
    )(rest, *mw)


def _mix_sample_kernel(r_ref, sc_ref, h0_ref, sp_ref, cw_ref, cb_ref, wa_ref, ba_ref, wx_ref, bx_ref,
                       lam_ref, wp_ref, ps_ref, y_ref, conv_ref, h_ref, pool_ref, *, c, pos0):
    xb = r_ref[:, 0:c]
    gb = r_ref[:, c:2 * c]
    u_in = r_ref[:, 2 * c:3 * c]

    xc = cb_ref[...] + cw_ref[CONV_W - 1:CONV_W, :] * xb
    for j in range(CONV_W - 1):
        xc = xc + cw_ref[j:j + 1, :] * sc_ref[j]
    a, u, gate = _rglru_gates(xc, gb, wa_ref, ba_ref, wx_ref, bx_ref, lam_ref)
    h = a * h0_ref[...] + u

    sums = {}
    acc = u_in
    for n in range(1, POOL_BUF + 1):
        acc = acc + sp_ref[POOL_BUF - n]
        if n + 1 in POOL_WINDOWS:
            sums[n + 1] = acc
    win = _pool_lane_select(u_in, *[sums[w] for w in POOL_WINDOWS])
    cnt = _pool_lane_select(u_in, *[float(min(pos0 + 1, w)) for w in POOL_WINDOWS])
    d = win / cnt - u_in
    y_pool = _dot(d.astype(BF16), wp_ref[...]) * ps_ref[...]

    y_ref[:, 0:c] = h * gate
    y_ref[:, c:2 * c] = y_pool
    h_ref[...] = h
    for j in range(CONV_W - 2):
        conv_ref[j] = sc_ref[j + 1]
    conv_ref[CONV_W - 2] = xb
    for j in range(POOL_BUF - 1):
        pool_ref[j] = sp_ref[j + 1]
    pool_ref[POOL_BUF - 1] = u_in


def _mix_sample(rest, sc_t, h0, sp_t, mw, *, pos0):
    bs, c3 = rest.shape
    c = c3 // 3
    return pl.pallas_call(
        functools.partial(_mix_sample_kernel, c=c, pos0=pos0), name="mix_sample",
        out_shape=[jax.ShapeDtypeStruct((bs, 2 * c), F32), jax.ShapeDtypeStruct(sc_t.shape, F32),
                   jax.ShapeDtypeStruct((bs, c), F32), jax.ShapeDtypeStruct(sp_t.shape, F32)],
        compiler_params=pltpu.CompilerParams(vmem_limit_bytes=VMEM_LIMIT),
    )(rest, sc_t, h0, sp_t, *mw)


def _decode_kernel(pt_ref, q_ref, kn_ref, vn_ref, f_ref, b_ref, lfp_ref, tri_ref, ck_hbm, cv_hbm,
                   o_ref, lfn_ref, kbuf, vbuf, ksem, vsem, *, layer, n_pages, page, n_heads):
    b = pl.program_id(0)
    nb = pl.num_programs(0)

    def copies(seq, slot):
        out = []
        for j in range(n_pages):
            pid = pt_ref[seq * n_pages + j]
            dst = pl.ds(j * page, page)
            out.append(pltpu.make_async_copy(ck_hbm.at[layer, pid], kbuf.at[slot, dst], ksem.at[slot]))
            out.append(pltpu.make_async_copy(cv_hbm.at[layer, pid], vbuf.at[slot, dst], vsem.at[slot]))
        return out

    slot = b % 2

    @pl.when(b == 0)
    def _():
        for cp in copies(0, 0):
            cp.start()

    @pl.when(b + 1 < nb)
    def _():
        for cp in copies(b + 1, 1 - slot):
            cp.start()

    d = q_ref.shape[-1]
    sub = lax.broadcasted_iota(jnp.int32, (n_heads, d), 0)
    lane = lax.broadcasted_iota(jnp.int32, (n_heads, d), 1)
    head_mask = (lane // HEAD_DIM) == sub

    lfn = -_softplus(-(f_ref[...] + b_ref[...]))
    lfn_ref[...] = lfn
    sub8 = lax.broadcasted_iota(jnp.int32, (n_heads, LANES), 0)
    lane8 = lax.broadcasted_iota(jnp.int32, (n_heads, LANES), 1)
    lfn_col = jnp.sum(jnp.where(sub8 == lane8, jnp.broadcast_to(lfn, (n_heads, LANES)), 0.0),
                      axis=-1, keepdims=True)

    pages = [lfp_ref[:, p * page:(p + 1) * page] for p in range(n_pages)]
    within = _dot3(jnp.concatenate(pages, axis=0), tri_ref[...])
    bias_blocks = [None] * n_pages
    carry = lfn_col
    for p in reversed(range(n_pages)):
        bias_blocks[p] = within[p * n_heads:(p + 1) * n_heads] + carry
        carry = carry + jnp.sum(pages[p], axis=-1, keepdims=True)
    bias = jnp.concatenate(bias_blocks, axis=1)

    q = q_ref[...]
    q_bd = jnp.where(head_mask, jnp.broadcast_to(q, (n_heads, d)), 0.0)
    s_new = jnp.sum(q_bd * kn_ref[...], axis=-1, keepdims=True)
    pad = jnp.zeros((n_heads, d), F32)
    q16 = jnp.concatenate([q_bd, pad], axis=0).astype(BF16)

    for cp in copies(b, slot):
        cp.wait()

    kb = kbuf[slot].astype(BF16)
    s = lax.dot_general(q16, kb, (((1,), (1,)), ((), ())), preferred_element_type=F32)[0:n_heads]
    s = s + bias
    m = jnp.maximum(jnp.max(s, axis=-1, keepdims=True), s_new)
    p_past = jnp.exp(s - m)
    p_new = jnp.exp(s_new - m)
    z = jnp.sum(p_past, axis=-1, keepdims=True) + p_new
    p16 = jnp.concatenate([p_past, jnp.zeros_like(p_past)], axis=0).astype(BF16)
    o = _dot(p16, vbuf[slot].astype(BF16))[0:n_heads]
    o = (o + p_new * vn_ref[...]) / z
    o_ref[...] = jnp.sum(jnp.where(head_mask, o, 0.0), axis=0, keepdims=True)


def _decode_attention(page_table, q, k_new, v_new, f_raw, b_pad, lfp_t, cache_k, cache_v, *, layer):
    bs, d = q.shape
    n_pages = page_table.shape[1]
    page = cache_k.shape[2]
    n_heads = d // HEAD_DIM
    p_len = n_pages * page
    tri = (jnp.arange(page)[:, None] > jnp.arange(page)[None, :]).astype(BF16)
    ck = cache_k.reshape(cache_k.shape[0], cache_k.shape[1], page, d)
    cv = cache_v.reshape(ck.shape)
    row = lambda n: pl.BlockSpec((None, 1, n), lambda i, pt: (i, 0, 0))
    kern = functools.partial(_decode_kernel, layer=layer, n_pages=n_pages, page=page, n_heads=n_heads)
    grid_spec = pltpu.PrefetchScalarGridSpec(
        num_scalar_prefetch=1, grid=(bs,),
        in_specs=[row(d), row(d), row(d), row(LANES),
                  pl.BlockSpec((1, LANES), lambda i, pt: (0, 0)),
                  pl.BlockSpec((None, n_heads, p_len), lambda i, pt: (i, 0, 0)),
                  pl.BlockSpec((page, page), lambda i, pt: (0, 0)),
                  pl.BlockSpec(memory_space=pl.ANY), pl.BlockSpec(memory_space=pl.ANY)],
        out_specs=[row(d), row(LANES)],
        scratch_shapes=[pltpu.VMEM((2, p_len, d), F32), pltpu.VMEM((2, p_len, d), F32),
                        pltpu.SemaphoreType.DMA((2,)), pltpu.SemaphoreType.DMA((2,))])
    o, lfn = pl.pallas_call(
        kern, grid_spec=grid_spec, name="decode_attention",
        out_shape=[jax.ShapeDtypeStruct((bs, 1, d), F32), jax.ShapeDtypeStruct((bs, 1, LANES), F32)],
        compiler_params=_params("arbitrary"),
    )(page_table.reshape(-1), q.reshape(bs, 1, d), k_new.reshape(bs, 1, d), v_new.reshape(bs, 1, d),
      f_raw.reshape(bs, 1, LANES), b_pad, lfp_t, tri, ck, cv)
    return o.reshape(bs, d), lfn.reshape(bs, LANES)


def _outproj_kernel(x_ref, o_ref, y_ref, ga_ref, gb_ref, woa_ref, wob_ref, gf_ref, wr_hi_ref, wr_lo_ref,
                    br_ref, x2_ref, xn2_ref, route_ref, *, att_padded, d_att, c):
    tm = x_ref.shape[0]
    if att_padded:
        o = o_ref[...]
        lane3 = lax.broadcasted_iota(jnp.int32, o.shape, 2)
        o = jnp.where(lane3 < HEAD_DIM, o, 0.0)
        ssq = jnp.sum(jnp.sum(o * o, axis=0), axis=-1, keepdims=True)
        inv = lax.rsqrt(ssq / d_att + EPS)
        y = jnp.zeros((tm, x_ref.shape[1]), F32)
        for h in range(o.shape[0]):
            y = y + _dot((o[h] * inv * ga_ref[h]).astype(BF16), woa_ref[h])
    else:
        y = _dot(_rms(o_ref[...], ga_ref[...]).astype(BF16), woa_ref[...])
    y = y + _dot(_rms(y_ref[:, 0:c], gb_ref[:, 0:c]).astype(BF16), wob_ref[0:c, :])
    y = y + _dot(_rms(y_ref[:, c:2 * c], gb_ref[:, c:2 * c]).astype(BF16), wob_ref[c:2 * c, :])
    x2 = x_ref[...] + y
    x2_ref[...] = x2
    xn2 = _rms(x2, gf_ref[...])
    xn2_ref[...] = xn2

    hi = xn2.astype(BF16)
    lo = (xn2 - hi.astype(F32)).astype(BF16)
    logit = _dot(hi, wr_hi_ref[...]) + _dot(lo, wr_hi_ref[...]) + _dot(hi, wr_lo_ref[...]) + br_ref[...]
    lane = lax.broadcasted_iota(jnp.int32, (tm, LANES), 1).astype(F32)
    far = float(LANES)

    def first_argmax(vals, valid):
        mx = jnp.max(jnp.where(valid, vals, NEG), axis=-1, keepdims=True)
        idx = jnp.min(jnp.where(valid & (vals == mx), lane, far), axis=-1, keepdims=True)
        return mx, idx

    gmask = lane < N_GROUPS
    gmax, gidx = first_argmax(logit, gmask)
    pg_top = 1.0 / jnp.sum(jnp.where(gmask, jnp.exp(logit - gmax), 0.0), axis=-1, keepdims=True)
    e_lo = N_GROUPS + EXPERTS_PER_GROUP * gidx
    emask = (lane >= e_lo) & (lane < e_lo + EXPERTS_PER_GROUP)
    m1, i1 = first_argmax(logit, emask)
    m2, i2 = first_argmax(logit, emask & (lane != i1))
    e = jnp.exp(m2 - m1)
    g1 = pg_top / (1.0 + e)
    g2 = pg_top * e / (1.0 + e)
    route_ref[...] = jnp.where(lane == 0, i1 - N_GROUPS, jnp.where(lane == 1, i2 - N_GROUPS,
                               jnp.where(lane == 2, g1, jnp.where(lane == 3, g2, 0.0))))


def _outproj(x, o, y, ow, *, tm, att_padded):
    b, t, d = x.shape
    c = y.shape[-1] // 2
    d_att = 512
    row = lambda n: pl.BlockSpec((None, tm, n), lambda i, j: (i, j, 0))
    const = lambda a: pl.BlockSpec(a.shape, lambda i, j: (0,) * a.ndim)
    if att_padded:
        o_spec = pl.BlockSpec((None, o.shape[1], tm, LANES), lambda i, j: (i, 0, j, 0))
    else:
        o_spec = row(d_att)
    kern = functools.partial(_outproj_kernel, att_padded=att_padded, d_att=d_att, c=c)
    return pl.pallas_call(
        kern, grid=(b, t // tm), name="outproj_heads" if att_padded else "outproj_flat",
        in_specs=[row(d), o_spec, row(2 * c)] + [const(a) for a in ow],
        out_specs=[row(d), row(d), row(LANES)],
        out_shape=[jax.ShapeDtypeStruct((b, t, d), F32), jax.ShapeDtypeStruct((b, t, d), F32),
                   jax.ShapeDtypeStruct((b, t, LANES), F32)],
        compiler_params=_params("parallel", "parallel"),
    )(x, o, y, *ow)


def _row_copy(src_hbm, row, buf, slot, r, sem):
    return pltpu.make_async_copy(src_hbm.at[pl.ds(row, 1)], buf.at[slot, pl.ds(r, 1)], sem.at[slot])


def _gather_rows(idx_ref, base, n, src_hbm, buf, slot, sem):
    def body(r, _):
        _row_copy(src_hbm, idx_ref[base + r], buf, slot, r, sem).start()
        return 0
    lax.fori_loop(0, n, body, 0)


def _wait_rows(n, src_hbm, buf, slot, sem):
    def body(r, _):
        _row_copy(src_hbm, 0, buf, slot, 0, sem).wait()
        return 0
    lax.fori_loop(0, n, body, 0)


def _moe_kernel(be_ref, src_ref, nu_ref, xn_hbm, wg_ref, wu_ref, wd_ref, y_ref,
                xbuf, sem, wgb, wub, wdb, *, blk):
    i = pl.program_id(0)
    n_used = nu_ref[0]
    slot = i % 2

    @pl.when((i == 0) & (n_used > 0))
    def _():
        _gather_rows(src_ref, 0, blk, xn_hbm, xbuf, 0, sem)

    @pl.when(i + 1 < n_used)
    def _():
        _gather_rows(src_ref, (i + 1) * blk, blk, xn_hbm, xbuf, 1 - slot, sem)

    @pl.when(i < n_used)
    def _():
        @pl.when((i == 0) | (be_ref[i] != be_ref[jnp.maximum(i - 1, 0)]))
        def _():
            wgb[...] = wg_ref[...].astype(BF16)
            wub[...] = wu_ref[...].astype(BF16)
            wdb[...] = wd_ref[...].astype(BF16)

        _wait_rows(blk, xn_hbm, xbuf, slot, sem)
        x = xbuf[slot].astype(BF16)
        g = _dot(x, wgb[...])
        u = _dot(x, wub[...])
        hmid = (g * _sigmoid(g) * u).astype(BF16)
        y_ref[...] = _dot(hmid, wdb[...])

    @pl.when(i >= n_used)
    def _():
        y_ref[...] = jnp.zeros_like(y_ref)


def _moe_experts(blk_e, src, n_used, xn_all, w_gate, w_up, w_down, *, layer, n_blocks):
    d = xn_all.shape[1]
    de = w_gate.shape[-1]
    blk = MOE_BLOCK
    wspec = lambda r, c_: pl.BlockSpec((None, None, r, c_), lambda i, be, s, nu: (layer, be[i], 0, 0))
    grid_spec = pltpu.PrefetchScalarGridSpec(
        num_scalar_prefetch=3, grid=(n_blocks,),
        in_specs=[pl.BlockSpec(memory_space=pl.ANY), wspec(d, de), wspec(d, de), wspec(de, d)],
        out_specs=pl.BlockSpec((blk, d), lambda i, be, s, nu: (i, 0)),
        scratch_shapes=[pltpu.VMEM((2, blk, d), F32), pltpu.SemaphoreType.DMA((2,)),
                        pltpu.VMEM((d, de), BF16), pltpu.VMEM((d, de), BF16), pltpu.VMEM((de, d), BF16)])
    return pl.pallas_call(
        functools.partial(_moe_kernel, blk=blk), grid_spec=grid_spec, name="moe_experts",
        out_shape=jax.ShapeDtypeStruct((n_blocks * blk, d), F32),
        compiler_params=_params("arbitrary"),
    )(blk_e, src, n_used, xn_all, w_gate, w_up, w_down)


def _dispatch(e_all, n_blocks):
    blk = MOE_BLOCK
    nk = e_all.shape[0] * TOP_K
    e_flat = e_all.reshape(nk)
    onehot = (e_flat[:, None] == jnp.arange(N_EXPERTS, dtype=jnp.int32)[None, :]).astype(jnp.int32)
    cum = jnp.cumsum(onehot, axis=0)
    counts = cum[-1]
    pcounts = (counts + blk - 1) // blk * blk
    pend = jnp.cumsum(pcounts)
    pstart = pend - pcounts
    dest = jnp.sum(onehot * (cum - 1 + pstart[None, :]), axis=1)
    src = jnp.zeros((n_blocks * blk,), jnp.int32).at[dest].set(jnp.arange(nk, dtype=jnp.int32) // TOP_K)
    n_used = (pend[-1] // blk).astype(jnp.int32)
    bidx = jnp.arange(n_blocks, dtype=jnp.int32)
    blk_e = jnp.searchsorted(pend, bidx * blk, side='right').astype(jnp.int32)
    blk_e = jnp.minimum(blk_e, N_EXPERTS - 1)
    last = blk_e[jnp.maximum(n_used - 1, 0)]
    blk_e = jnp.where(bidx < n_used, blk_e, last)
    return blk_e, src, n_used.reshape(1), dest.reshape(-1, TOP_K)


def _combine_kernel(d_ref, x2_ref, route_ref, g_ref, y_hbm, o_ref, ybuf, sem, *, tm, final_norm):
    i = pl.program_id(0)
    n = pl.num_programs(0)
    slot = i % 2
    rows = TOP_K * tm

    @pl.when(i == 0)
    def _():
        _gather_rows(d_ref, 0, rows, y_hbm, ybuf, 0, sem)

    @pl.when(i + 1 < n)
    def _():
        _gather_rows(d_ref, (i + 1) * rows, rows, y_hbm, ybuf, 1 - slot, sem)

    _wait_rows(rows, y_hbm, ybuf, slot, sem)
    route = route_ref[...]
    out = x2_ref[...] + (route[:, 2:3] * ybuf[slot, 0:tm] + route[:, 3:4] * ybuf[slot, tm:2 * tm])
    if final_norm:
        out = _rms(out, g_ref[...])
    o_ref[...] = out


def _combine(x2, route, dest, ybuf_hbm, g_final, *, tm, final_norm):
    n, d = x2.shape
    nt = n // tm
    d_tiles = dest.reshape(nt, tm, TOP_K).transpose(0, 2, 1).reshape(-1)
    grid_spec = pltpu.PrefetchScalarGridSpec(
        num_scalar_prefetch=1, grid=(nt,),
        in_specs=[pl.BlockSpec((tm, d), lambda i, dr: (i, 0)),
                  pl.BlockSpec((tm, LANES), lambda i, dr: (i, 0)),
                  pl.BlockSpec((1, d), lambda i, dr: (0, 0)),
                  pl.BlockSpec(memory_space=pl.ANY)],
        out_specs=pl.BlockSpec((tm, d), lambda i, dr: (i, 0)),
        scratch_shapes=[pltpu.VMEM((2, TOP_K * tm, d), F32), pltpu.SemaphoreType.DMA((2,))])
    return pl.pallas_call(
        functools.partial(_combine_kernel, tm=tm, final_norm=final_norm), grid_spec=grid_spec, name="combine",
        out_shape=jax.ShapeDtypeStruct((n, d), F32),
        compiler_params=_params("arbitrary"),
    )(d_tiles, x2, route, g_final, ybuf_hbm)


def _block_diag(w):
    n, a, b = w.shape
    out = jnp.zeros((n * a, n * b), w.dtype)
    for i in range(n):
        out = out.at[i * a:(i + 1) * a, i * b:(i + 1) * b].set(w[i])
    return out


def _pad_lanes(v, n=LANES):
    return jnp.zeros((1, n), F32).at[0, :v.shape[0]].set(v)


def _tile_for(t, pref):
    tm = min(pref, t)
    while t % tm:
        tm //= 2
    return tm


def kernel(x_prompt, x_sample, cache_k, cache_v, cache_logf, state_conv, state_rglru, state_pool, page_table, norm_mix, w_in, b_forget, conv_w, conv_b, w_gate_a, b_gate_a, w_gate_x, b_gate_x, lru_lambda, w_pool, pool_scale, norm_groups, w_out, norm_ffn, w_router_group, b_router_group, w_router_expert, b_router_expert, w_up, w_gate, w_down, norm_final):
    depth = w_in.shape[0]
    b_p, t_p, d = x_prompt.shape
    b_s = x_sample.shape[0]
    n_heads = b_forget.shape[1]
    d_att = n_heads * HEAD_DIM
    c = conv_w.shape[-1]
    n_p = b_p * t_p
    n_all = n_p + b_s
    n_pages, page = page_table.shape[1], cache_k.shape[2]
    past_len = n_pages * page
    nk = n_all * TOP_K
    n_blocks = -(-(nk + N_EXPERTS * (MOE_BLOCK - 1)) // MOE_BLOCK)

    tm_p = _tile_for(t_p, 512)
    tq = _tile_for(t_p, 512)
    tr = _tile_for(t_p, 256)
    tm_c = _tile_for(n_p, 256)

    xp = x_prompt
    xs = x_sample.reshape(1, b_s, d)
    off_f = 3 * d_att
    outs = [[] for _ in range(12)]
    y_prompt = y_sample = None
    for l in range(depth):
        wl = w_in[l]
        w_cat = jnp.concatenate(
            [wl[:, :off_f], wl[:, off_f + n_heads:], wl[:, off_f:off_f + n_heads],
             jnp.zeros((d, LANES - n_heads), F32)], axis=1).astype(BF16)
        g_mix = norm_mix[l].reshape(1, d)
        b_pad = _pad_lanes(b_forget[l])
        mw = (conv_w[l], conv_b[l].reshape(1, c), _block_diag(w_gate_a[l]).astype(BF16),
              b_gate_a[l].reshape(1, c), _block_diag(w_gate_x[l]).astype(BF16), b_gate_x[l].reshape(1, c),
              lru_lambda[l].reshape(1, c), _block_diag(w_pool[l]).astype(BF16), pool_scale[l].reshape(1, c))
        ng, wo = norm_groups[l], w_out[l]
        ga_pad = jnp.zeros((n_heads, 1, LANES), F32).at[:, 0, :HEAD_DIM].set(ng[:d_att].reshape(n_heads, HEAD_DIM))
        woa_pad = jnp.zeros((n_heads, LANES, d), F32).at[:, :HEAD_DIM].set(
            wo[:d_att].reshape(n_heads, HEAD_DIM, d)).astype(BF16)
        w_r = jnp.zeros((d, LANES), F32).at[:, :N_GROUPS].set(w_router_group[l]).at[
            :, N_GROUPS:N_GROUPS + N_EXPERTS].set(w_router_expert[l])
        wr_hi = w_r.astype(BF16)
        wr_lo = (w_r - wr_hi.astype(F32)).astype(BF16)
        b_r = _pad_lanes(jnp.concatenate([b_router_group[l], b_router_expert[l]]))
        ow_tail = (ng[d_att:].reshape(1, 2 * c), None, wo[d_att:].astype(BF16), norm_ffn[l].reshape(1, d),
                   wr_hi, wr_lo, b_r)
        ow_p = (ga_pad, ow_tail[0], woa_pad, *ow_tail[2:])
        ow_s = (ng[:d_att].reshape(1, d_att), ow_tail[0], wo[:d_att].astype(BF16), *ow_tail[2:])

        qp, kp, vp, k_p, v_p, f_p, rest_p = _inproj(xp, g_mix, w_cat, tm=tm_p, head_layout=True)
        lf_p, nct = _forget(f_p, b_pad, tc=tm_p, n_heads=n_heads)
        o_p = _attention(qp, kp, vp, nct, tq=tq)
        y_mix_p, conv_t, h_t, pool_t = _mix_prompt(rest_p, mw, tr=tr)
        x2_p, xn2_p, route_p = _outproj(xp, o_p, y_mix_p, ow_p, tm=tm_p, att_padded=True)

        q_s, k_s, v_s, f_s, rest_s = _inproj(xs, g_mix, w_cat, tm=b_s, head_layout=False)
        lfp_t = cache_logf[l][page_table].reshape(b_s, past_len, n_heads).transpose(0, 2, 1)
        o_s, lf_s = _decode_attention(page_table, q_s[0], k_s[0], v_s[0], f_s[0], b_pad, lfp_t,
                                      cache_k, cache_v, layer=l)
        y_mix_s, conv_s, h_s, pool_s = _mix_sample(
            rest_s[0], state_conv[l].transpose(1, 0, 2), state_rglru[l], state_pool[l].transpose(1, 0, 2),
            mw, pos0=past_len)
        x2_s, xn2_s, route_s = _outproj(xs, o_s[None], y_mix_s[None], ow_s, tm=b_s, att_padded=False)
        xn2_all = jnp.concatenate([xn2_p.reshape(n_p, d), xn2_s.reshape(b_s, d)], axis=0)

        route_all = jnp.concatenate([route_p.reshape(n_p, LANES), route_s.reshape(b_s, LANES)], axis=0)
        e_all = route_all[:, 0:TOP_K].astype(jnp.int32)
        blk_e, src, n_used, dest = _dispatch(e_all, n_blocks)
        ybuf = _moe_experts(blk_e, src, n_used, xn2_all, w_gate, w_up, w_down, layer=l, n_blocks=n_blocks)
        last = l == depth - 1
        g_fin = norm_final.reshape(1, d)
        xp = _combine(x2_p.reshape(n_p, d), route_all[:n_p], dest[:n_p], ybuf, g_fin,
                      tm=tm_c, final_norm=last).reshape(b_p, t_p, d)
        xs = _combine(x2_s.reshape(b_s, d), route_all[n_p:], dest[n_p:], ybuf, g_fin,
                      tm=b_s, final_norm=last).reshape(1, b_s, d)

        per_layer = (
            k_p.reshape(b_p, t_p, n_heads, HEAD_DIM), v_p.reshape(b_p, t_p, n_heads, HEAD_DIM),
            lf_p[:, :, :n_heads], conv_t[:, 8 - (CONV_W - 1):], h_t[:, 7], pool_t[:, 16 - POOL_BUF:],
            k_s.reshape(b_s, 1, n_heads, HEAD_DIM), v_s.reshape(b_s, 1, n_heads, HEAD_DIM),
            lf_s[:, None, :n_heads], conv_s.transpose(1, 0, 2), h_s, pool_s.transpose(1, 0, 2))
        for acc, val in zip(outs, per_layer):
            acc.append(val)
    y_prompt = xp
    y_sample = xs.reshape(b_s, 1, d)
    return (y_prompt, y_sample, *[jnp.stack(o) for o in outs])
```

```python
import functools

import jax
import jax.numpy as jnp
from jax import lax
from jax.experimental import pallas as pl
from jax.experimental.pallas import tpu as pltpu

F32 = jnp.float32
BF16 = jnp.bfloat16

EPS = 1e-6
HEAD_DIM = 64
LANES = 128
RG_C = 8.0
CONV_W = 4
POOL_WINDOWS = (2, 4, 8, 16)
POOL_BUF = max(POOL_WINDOWS) - 1
N_GROUPS = 4
EXPERTS_PER_GROUP = 8
N_EXPERTS = N_GROUPS * EXPERTS_PER_GROUP
TOP_K = 2
MOE_BLOCK = 256
DMA_UNROLL = 8
NEG = -1e30
LOG2E = 1.4426950408889634
N_BIAS = 3
VMEM_LIMIT = 56 * 1024 * 1024


def _params(*sem):
    return pltpu.CompilerParams(dimension_semantics=sem, vmem_limit_bytes=VMEM_LIMIT)


def _split3(x):
    hi = x.astype(BF16)
    r1 = x - hi.astype(F32)
    mid = r1.astype(BF16)
    lo = (r1 - mid.astype(F32)).astype(BF16)
    return hi, mid, lo


def _dot(a, b):
    return jnp.dot(a, b, preferred_element_type=F32)


def _dot_nt(a, b):
    return lax.dot_general(a, b, (((1,), (1,)), ((), ())), preferred_element_type=F32)


def _dot3(x, w_bf16):
    hi, mid, lo = _split3(x)
    return _dot(hi, w_bf16) + _dot(mid, w_bf16) + _dot(lo, w_bf16)


def _dot3_left(w_bf16, x):
    hi, mid, lo = _split3(x)
    return _dot(w_bf16, hi) + _dot(w_bf16, mid) + _dot(w_bf16, lo)


def _sigmoid(x):
    return 1.0 / (1.0 + jnp.exp(-x))


def _softplus(x):
    return jnp.maximum(x, 0.0) + jnp.log1p(jnp.exp(-jnp.abs(x)))


def _gelu_tanh(x):
    return 0.5 * x * (1.0 + jnp.tanh(0.7978845608028654 * (x + 0.044715 * (x * x * x))))


def _rms(x, g):
    return x * lax.rsqrt(jnp.mean(x * x, axis=-1, keepdims=True) + EPS) * g


def _moe_combine(x2, route, y):
    d = x2.shape[-1]
    return x2 + (route[:, 2:3] * y[:, 0:d] + route[:, 3:4] * y[:, d:2 * d])


def _inproj_kernel(*refs, d_att, d_rest, head_layout, has_moe, scale):
    refs = list(refs)
    x_ref = refs.pop(0)
    if has_moe:
        y_ref, route_ref = refs.pop(0), refs.pop(0)
    g_ref, w_ref = refs.pop(0), refs.pop(0)
    if head_layout:
        b_ref, tri_ref, place_ref = refs.pop(0), refs.pop(0), refs.pop(0)
    x = x_ref[...]
    if has_moe:
        x = _moe_combine(x, route_ref[...], y_ref[...])
        refs.pop(0)[...] = x
    xn = _rms(x, g_ref[...]).astype(BF16)

    def sec(lo, n):
        return _dot(xn, w_ref[:, lo:lo + n])

    q = sec(0, d_att) * scale
    k = sec(d_att, d_att)
    v = sec(2 * d_att, d_att)
    rest = sec(3 * d_att, d_rest)
    f = sec(3 * d_att + d_rest, LANES)
    if not head_layout:
        q_ref, k_ref, v_ref, f_ref, r_ref = refs
        q_ref[...] = q
        k_ref[...] = k
        v_ref[...] = v
        f_ref[...] = f
        r_ref[...] = rest
        return
    qa_ref, ka_ref, vp_ref, k_ref, v_ref, lf_ref, r_ref, carry_ref = refs
    k_ref[...] = k
    v_ref[...] = v
    r_ref[...] = rest

    @pl.when(pl.program_id(1) == 0)
    def _():
        carry_ref[...] = jnp.zeros_like(carry_ref)

    lf = -_softplus(-(f + b_ref[...]))
    lf_ref[...] = lf
    cs = _dot3_left(tri_ref[...], lf) + carry_ref[...]
    carry_ref[...] = carry_ref[...] + jnp.sum(lf, axis=0, keepdims=True)

    tm = q.shape[0]
    n_heads = d_att // HEAD_DIM
    lane = lax.broadcasted_iota(jnp.int32, (tm, LANES), 1)
    parts = [p.astype(F32) for p in _split3(cs * (-LOG2E))]
    packed = jnp.where(lane < n_heads, parts[0], 0.0)
    for j in range(1, N_BIAS):
        in_j = (lane >= j * n_heads) & (lane < (j + 1) * n_heads)
        packed = packed + jnp.where(in_j, pltpu.roll(parts[j], j * n_heads, axis=1), 0.0)
    aug = _dot(packed.astype(BF16), place_ref[...])

    low = lane < HEAD_DIM
    q_fill = jnp.where((lane >= HEAD_DIM) & (lane < HEAD_DIM + N_BIAS), 1.0, 0.0)
    v_fill = jnp.where(lane == HEAD_DIM, 1.0, 0.0)
    for j in range(d_att // LANES):
        sl = slice(j * LANES, (j + 1) * LANES)
        for val, ref, fills in ((q, qa_ref, (q_fill, q_fill)), (v, vp_ref, (v_fill, v_fill)),
                                (k, ka_ref, (aug[:, 2 * j * LANES:(2 * j + 1) * LANES],
                                             aug[:, (2 * j + 1) * LANES:(2 * j + 2) * LANES]))):
            pair = val[:, sl]
            ref[2 * j] = jnp.where(low, pair, fills[0]).astype(BF16)
            ref[2 * j + 1] = jnp.where(low, pltpu.roll(pair, HEAD_DIM, axis=1), fills[1]).astype(BF16)


def _bias_placement(n_heads):
    rows = jnp.arange(LANES)[:, None]
    cols = jnp.arange(n_heads * LANES)[None, :]
    j, h = rows // n_heads, rows % n_heads
    hit = (rows < N_BIAS * n_heads) & (cols == h * LANES + HEAD_DIM + j)
    return hit.astype(BF16)


def _inproj(x, g, w, *, tm, head_layout, moe=None, b_pad=None, y_block_offset=0):
    b, t, d = x.shape
    n_w = w.shape[1]
    d_att = 512
    d_rest = n_w - 3 * d_att - LANES
    n_heads = d_att // HEAD_DIM
    nt = t // tm
    row = lambda n: pl.BlockSpec((None, tm, n), lambda i, j: (i, j, 0))
    const = lambda a: pl.BlockSpec(a.shape, lambda i, j: (0,) * a.ndim)
    slab = pl.BlockSpec((None, n_heads, tm, LANES), lambda i, j: (i, 0, j, 0))
    f32s = lambda n: jax.ShapeDtypeStruct((b, t, n), F32)
    args, in_specs = [x], [row(d)]
    out_shape, out_specs = [], []
    if moe is not None:
        y, route = moe
        args += [y, route]
        in_specs += [pl.BlockSpec((tm, 2 * d), lambda i, j: (y_block_offset + i * nt + j, 0)), row(LANES)]
        out_shape.append(f32s(d))
        out_specs.append(row(d))
    args += [g, w]
    in_specs += [const(g), const(w)]
    scratch = []
    if head_layout:
        tri = (jnp.arange(tm)[:, None] >= jnp.arange(tm)[None, :]).astype(BF16)
        place = _bias_placement(n_heads)
        args += [b_pad, tri, place]
        in_specs += [const(b_pad), const(tri), const(place)]
        slab_s = jax.ShapeDtypeStruct((b, n_heads, t, LANES), BF16)
        out_shape += [slab_s, slab_s, slab_s]
        out_specs += [slab, slab, slab]
        scratch = [pltpu.VMEM((1, LANES), F32)]
        scale = HEAD_DIM ** -0.5 * LOG2E
    else:
        out_shape.append(f32s(d_att))
        out_specs.append(row(d_att))
        scale = HEAD_DIM ** -0.5
    out_shape += [f32s(d_att), f32s(d_att), f32s(LANES), f32s(d_rest)]
    out_specs += [row(d_att), row(d_att), row(LANES), row(d_rest)]
    kern = functools.partial(_inproj_kernel, d_att=d_att, d_rest=d_rest, head_layout=head_layout,
                             has_moe=moe is not None, scale=scale)
    return pl.pallas_call(
        kern, grid=(b, nt), name="inproj_heads" if head_layout else "inproj_flat",
        in_specs=in_specs, out_specs=out_specs, out_shape=out_shape, scratch_shapes=scratch,
        compiler_params=_params("parallel", "arbitrary"),
    )(*args)


def _attention_kernel(q_ref, k_ref, v_ref, o_ref, *, tk, n_chains):
    qi = pl.program_id(2)
    qs = [q_ref[c * tk:(c + 1) * tk, :] for c in range(n_chains)]

    def kv(j):
        off = pl.multiple_of(j * tk, tk)
        return k_ref[pl.ds(off, tk), :], v_ref[pl.ds(off, tk), :]

    def tile(q, k, v, m, acc, masked):
        s = _dot_nt(q, k)
        if masked:
            row = lax.broadcasted_iota(jnp.int32, (tk, tk), 0)
            col = lax.broadcasted_iota(jnp.int32, (tk, tk), 1)
            s = jnp.where(row >= col, s, NEG)
        m_new = jnp.maximum(m, jnp.max(s, axis=-1, keepdims=True))
        p = jnp.exp2(s - m_new)
        acc = jnp.exp2(m - m_new) * acc + _dot(p.astype(BF16), v)
        return m_new, acc

    def body(j, carry):
        for d in range(n_chains):
            k, v = kv(j * n_chains + d)
            carry = tuple(tile(qs[c], k, v, *carry[c], False) for c in range(n_chains))
        return carry

    init = tuple((jnp.full((tk, 1), NEG, F32), jnp.zeros((tk, LANES), F32)) for _ in range(n_chains))
    carry = list(lax.fori_loop(0, qi, body, init))
    for d in range(n_chains):
        k, v = kv(qi * n_chains + d)
        for c in range(d, n_chains):
            carry[c] = tile(qs[c], k, v, *carry[c], c == d)
    for c in range(n_chains):
        acc = carry[c][1]
        o_ref[c * tk:(c + 1) * tk, :] = acc / acc[:, HEAD_DIM:HEAD_DIM + 1]


def _attention(qa, ka, vp, *, tk, n_chains):
    b, h, t, _ = qa.shape
    tq = tk * n_chains
    tile = pl.BlockSpec((None, None, tq, LANES), lambda i, j, n: (i, j, n, 0))
    full = pl.BlockSpec((None, None, t, LANES), lambda i, j, n: (i, j, 0, 0))
    return pl.pallas_call(
        functools.partial(_attention_kernel, tk=tk, n_chains=n_chains), grid=(b, h, t // tq), name="attention",
        in_specs=[tile, full, full], out_specs=tile,
        out_shape=jax.ShapeDtypeStruct((b, h, t, LANES), F32),
        compiler_params=_params("parallel", "parallel", "arbitrary"),
    )(qa, ka, vp)


def _rglru_gates(xc, gb, wa_ref, ba_ref, wx_ref, bx_ref, lam_ref):
    xcb = xc.astype(BF16)
    r = _sigmoid(_dot(xcb, wa_ref[...]) + ba_ref[...])
    i = _sigmoid(_dot(xcb, wx_ref[...]) + bx_ref[...])
    log_a = -RG_C * r * _softplus(-lam_ref[...])
    a = jnp.exp(log_a)
    u = jnp.sqrt(jnp.tanh(-log_a) * (a * a + 1.0)) * (i * xc)
    return a, u, _gelu_tanh(gb)


def _pool_lane_select(c, w2, w4, w8, w16):
    lane = lax.broadcasted_iota(jnp.int32, c.shape, 1)
    gw = c.shape[1] // len(POOL_WINDOWS)
    return jnp.where(lane < gw, w2, jnp.where(lane < 2 * gw, w4, jnp.where(lane < 3 * gw, w8, w16)))


def _mix_prompt_kernel(r_ref, cw_ref, cb_ref, wa_ref, ba_ref, wx_ref, bx_ref, lam_ref, wp_ref, ps_ref,
                       y_ref, conv_ref, h_ref, pool_ref, xprev_ref, uprev_ref, htail_ref, *, tr, c):
    t = pl.program_id(1)

    @pl.when(t == 0)
    def _():
        xprev_ref[...] = jnp.zeros_like(xprev_ref)
        uprev_ref[...] = jnp.zeros_like(uprev_ref)
        htail_ref[...] = jnp.zeros_like(htail_ref)

    xb = r_ref[:, 0:c]
    gb = r_ref[:, c:2 * c]
    u_in = r_ref[:, 2 * c:3 * c]

    ext = jnp.concatenate([xprev_ref[...], xb], axis=0)
    xc = cb_ref[...] + cw_ref[CONV_W - 1:CONV_W, :] * xb
    for j in range(1, CONV_W):
        xc = xc + cw_ref[CONV_W - 1 - j:CONV_W - j, :] * pltpu.roll(ext, j, axis=0)[8:]

    a, u, gate = _rglru_gates(xc, gb, wa_ref, ba_ref, wx_ref, bx_ref, lam_ref)

    row = lax.broadcasted_iota(jnp.int32, (tr, c), 0)
    big_a, big_u = a, u
    sh = 1
    while sh < tr:
        keep = row >= sh
        a_s = jnp.where(keep, pltpu.roll(big_a, sh, axis=0), 1.0)
        u_s = jnp.where(keep, pltpu.roll(big_u, sh, axis=0), 0.0)
        big_u = big_a * u_s + big_u
        big_a = big_a * a_s
        sh *= 2
    h = big_u + big_a * htail_ref[7:8, :]

    extu = jnp.concatenate([uprev_ref[...], u_in], axis=0)
    s2 = extu + pltpu.roll(extu, 1, axis=0)
    s4 = s2 + pltpu.roll(s2, 2, axis=0)
    s8 = s4 + pltpu.roll(s4, 4, axis=0)
    s16 = s8 + pltpu.roll(s8, 8, axis=0)
    win = _pool_lane_select(u_in, s2[16:], s4[16:], s8[16:], s16[16:])
    wsz = _pool_lane_select(u_in, 2.0, 4.0, 8.0, 16.0)
    pos = (t * tr + row).astype(F32)
    d = win / jnp.minimum(pos + 1.0, wsz) - u_in
    y_pool = _dot(d.astype(BF16), wp_ref[...]) * ps_ref[...]

    y_ref[:, 0:c] = h * gate
    y_ref[:, c:2 * c] = y_pool
    conv_ref[...] = xb[tr - 8:]
    h_ref[...] = h[tr - 8:]
    pool_ref[...] = u_in[tr - 16:]
    xprev_ref[...] = xb[tr - 8:]
    uprev_ref[...] = u_in[tr - 16:]
    htail_ref[...] = h[tr - 8:]


def _mix_prompt(rest, mw, *, tr):
    b, t, c3 = rest.shape
    c = c3 // 3
    vec = lambda n: pl.BlockSpec((n, c), lambda i, j: (0, 0))
    mat = pl.BlockSpec((c, c), lambda i, j: (0, 0))
    tail = lambda n: pl.BlockSpec((None, n, c), lambda i, j: (i, 0, 0))
    return pl.pallas_call(
        functools.partial(_mix_prompt_kernel, tr=tr, c=c), grid=(b, t // tr), name="mix_prompt",
        in_specs=[pl.BlockSpec((None, tr, c3), lambda i, j: (i, j, 0)),
                  vec(CONV_W), vec(1), mat, vec(1), mat, vec(1), vec(1), mat, vec(1)],
        out_specs=[pl.BlockSpec((None, tr, 2 * c), lambda i, j: (i, j, 0)), tail(8), tail(8), tail(16)],
        out_shape=[jax.ShapeDtypeStruct((b, t, 2 * c), F32), jax.ShapeDtypeStruct((b, 8, c), F32),
                   jax.ShapeDtypeStruct((b, 8, c), F32), jax.ShapeDtypeStruct((b, 16, c), F32)],
        scratch_shapes=[pltpu.VMEM((8, c), F32), pltpu.VMEM((16, c), F32), pltpu.VMEM((8, c), F32)],
        compiler_params=_params("parallel", "arbitrary"),
    )(rest, *mw)


def _mix_sample_kernel(r_ref, sc_ref, h0_ref, sp_ref, cw_ref, cb_ref, wa_ref, ba_ref, wx_ref, bx_ref,
                       lam_ref, wp_ref, ps_ref, y_ref, conv_ref, h_ref, pool_ref, *, c, pos0):
    xb = r_ref[:, 0:c]
    gb = r_ref[:, c:2 * c]
    u_in = r_ref[:, 2 * c:3 * c]

    xc = cb_ref[...] + cw_ref[CONV_W - 1:CONV_W, :] * xb
    for j in range(CONV_W - 1):
        xc = xc + cw_ref[j:j + 1, :] * sc_ref[j]
    a, u, gate = _rglru_gates(xc, gb, wa_ref, ba_ref, wx_ref, bx_ref, lam_ref)
    h = a * h0_ref[...] + u

    sums = {}
    acc = u_in
    for n in range(1, POOL_BUF + 1):
        acc = acc + sp_ref[POOL_BUF - n]
        if n + 1 in POOL_WINDOWS:
            sums[n + 1] = acc
    win = _pool_lane_select(u_in, *[sums[w] for w in POOL_WINDOWS])
    cnt = _pool_lane_select(u_in, *[float(min(pos0 + 1, w)) for w in POOL_WINDOWS])
    d = win / cnt - u_in
    y_pool = _dot(d.astype(BF16), wp_ref[...]) * ps_ref[...]

    y_ref[:, 0:c] = h * gate
    y_ref[:, c:2 * c] = y_pool
    h_ref[...] = h
    for j in range(CONV_W - 2):
        conv_ref[j] = sc_ref[j + 1]
    conv_ref[CONV_W - 2] = xb
    for j in range(POOL_BUF - 1):
        pool_ref[j] = sp_ref[j + 1]
    pool_ref[POOL_BUF - 1] = u_in


def _mix_sample(rest, sc_t, h0, sp_t, mw, *, pos0):
    bs, c3 = rest.shape
    c = c3 // 3
    return pl.pallas_call(
        functools.partial(_mix_sample_kernel, c=c, pos0=pos0), name="mix_sample",
        out_shape=[jax.ShapeDtypeStruct((bs, 2 * c), F32), jax.ShapeDtypeStruct(sc_t.shape, F32),
                   jax.ShapeDtypeStruct((bs, c), F32), jax.ShapeDtypeStruct(sp_t.shape, F32)],
        compiler_params=pltpu.CompilerParams(vmem_limit_bytes=VMEM_LIMIT),
    )(rest, sc_t, h0, sp_t, *mw)


def _decode_kernel(pt_ref, q_ref, kn_ref, vn_ref, f_ref, b_ref, lfp_ref, tri_ref, ck_hbm, cv_hbm,
                   o_ref, lfn_ref, kbuf, vbuf, ksem, vsem, *, layer, n_pages, page, n_heads):
    b = pl.program_id(0)
    nb = pl.num_programs(0)

    def copies(seq, slot):
        out = []
        for j in range(n_pages):
            pid = pt_ref[seq * n_pages + j]
            out.append(pltpu.make_async_copy(ck_hbm.at[layer, pid], kbuf.at[slot, j], ksem.at[slot]))
            out.append(pltpu.make_async_copy(cv_hbm.at[layer, pid], vbuf.at[slot, j], vsem.at[slot]))
        return out

    slot = b % 2

    @pl.when(b == 0)
    def _():
        for cp in copies(0, 0):
            cp.start()

    @pl.when(b + 1 < nb)
    def _():
        for cp in copies(b + 1, 1 - slot):
            cp.start()

    lfn = -_softplus(-(f_ref[...] + b_ref[...]))
    lfn_ref[...] = lfn
    sub8 = lax.broadcasted_iota(jnp.int32, (n_heads, LANES), 0)
    lane8 = lax.broadcasted_iota(jnp.int32, (n_heads, LANES), 1)
    lfn_col = jnp.sum(jnp.where(sub8 == lane8, jnp.broadcast_to(lfn, (n_heads, LANES)), 0.0),
                      axis=-1, keepdims=True)

    pages = [lfp_ref[:, p * page:(p + 1) * page] for p in range(n_pages)]
    within = _dot3(jnp.concatenate(pages, axis=0), tri_ref[...])
    blocks = [None] * n_pages
    carry = lfn_col
    for p in reversed(range(n_pages)):
        blocks[p] = within[p * n_heads:(p + 1) * n_heads] + carry
        carry = carry + jnp.sum(pages[p], axis=-1, keepdims=True)
    bias = jnp.concatenate(blocks, axis=1)

    q = q_ref[...]
    s_new = jnp.sum(q * kn_ref[...], axis=-1, keepdims=True)
    q16 = jnp.concatenate([q, jnp.zeros_like(q)], axis=0).astype(BF16)
    p_len = n_pages * page
    head_row = lax.broadcasted_iota(jnp.int32, (n_heads, p_len), 0)

    for cp in copies(b, slot):
        cp.wait()

    def head_mat(buf, h):
        return jnp.concatenate([buf[slot, j, h] for j in range(n_pages)], axis=1).astype(BF16)

    s = bias
    for h in range(n_heads):
        s_h = _dot(q16, head_mat(kbuf, h))[0:n_heads]
        s = s + jnp.where(head_row == h, s_h, 0.0)
    m = jnp.maximum(jnp.max(s, axis=-1, keepdims=True), s_new)
    p_past = jnp.exp(s - m)
    p_new = jnp.exp(s_new - m)
    z = jnp.sum(p_past, axis=-1, keepdims=True) + p_new
    p16 = jnp.concatenate([p_past, jnp.zeros_like(p_past)], axis=0).astype(BF16)
    out_row = lax.broadcasted_iota(jnp.int32, (n_heads, HEAD_DIM), 0)
    o = p_new * vn_ref[...]
    for h in range(n_heads):
        o_h = _dot_nt(p16, head_mat(vbuf, h))[0:n_heads]
        o = o + jnp.where(out_row == h, o_h, 0.0)
    o_ref[...] = o / z


def _decode_attention(page_table, q, k_new, v_new, f_raw, b_pad, lfp_t, cache_kt, cache_vt, *, layer):
    bs, n_heads, _ = q.shape
    n_pages = page_table.shape[1]
    page = cache_kt.shape[-1]
    p_len = n_pages * page
    tri = (jnp.arange(page)[:, None] > jnp.arange(page)[None, :]).astype(BF16)
    hrow = pl.BlockSpec((None, n_heads, HEAD_DIM), lambda i, pt: (i, 0, 0))
    frow = pl.BlockSpec((None, 1, LANES), lambda i, pt: (i, 0, 0))
    const = lambda a: pl.BlockSpec(a.shape, lambda i, pt: (0,) * a.ndim)
    kern = functools.partial(_decode_kernel, layer=layer, n_pages=n_pages, page=page, n_heads=n_heads)
    buf = pltpu.VMEM((2, n_pages, n_heads, HEAD_DIM, page), F32)
    grid_spec = pltpu.PrefetchScalarGridSpec(
        num_scalar_prefetch=1, grid=(bs,),
        in_specs=[hrow, hrow, hrow, frow, const(b_pad),
                  pl.BlockSpec((None, n_heads, p_len), lambda i, pt: (i, 0, 0)), const(tri),
                  pl.BlockSpec(memory_space=pl.ANY), pl.BlockSpec(memory_space=pl.ANY)],
        out_specs=[hrow, frow],
        scratch_shapes=[buf, buf, pltpu.SemaphoreType.DMA((2,)), pltpu.SemaphoreType.DMA((2,))])
    o, lfn = pl.pallas_call(
        kern, grid_spec=grid_spec, name="decode_attention",
        out_shape=[jax.ShapeDtypeStruct((bs, n_heads, HEAD_DIM), F32), jax.ShapeDtypeStruct((bs, 1, LANES), F32)],
        compiler_params=_params("arbitrary"),
    )(page_table.reshape(-1), q, k_new, v_new, f_raw.reshape(bs, 1, LANES), b_pad, lfp_t, tri,
      cache_kt, cache_vt)
    return o, lfn.reshape(bs, LANES)


def _outproj_kernel(x_ref, o_ref, y_ref, ga_ref, gb_ref, woa_ref, wob_ref, gf_ref, wr_hi_ref, wr_lo_ref,
                    br_ref, x2_ref, xn2_ref, route_ref, *, att_padded, d_att, c):
    tm = x_ref.shape[0]
    if att_padded:
        o = o_ref[...]
        lane3 = lax.broadcasted_iota(jnp.int32, o.shape, 2)
        o = jnp.where(lane3 < HEAD_DIM, o, 0.0)
        ssq = jnp.sum(jnp.sum(o * o, axis=0), axis=-1, keepdims=True)
        inv = lax.rsqrt(ssq / d_att + EPS)
        y = jnp.zeros((tm, x_ref.shape[1]), F32)
        for h in range(o.shape[0]):
            y = y + _dot((o[h] * inv * ga_ref[h]).astype(BF16), woa_ref[h])
    else:
        y = _dot(_rms(o_ref[...], ga_ref[...]).astype(BF16), woa_ref[...])
    y = y + _dot(_rms(y_ref[:, 0:c], gb_ref[:, 0:c]).astype(BF16), wob_ref[0:c, :])
    y = y + _dot(_rms(y_ref[:, c:2 * c], gb_ref[:, c:2 * c]).astype(BF16), wob_ref[c:2 * c, :])
    x2 = x_ref[...] + y
    x2_ref[...] = x2
    xn2 = _rms(x2, gf_ref[...])
    xn2_ref[...] = xn2

    hi = xn2.astype(BF16)
    lo = (xn2 - hi.astype(F32)).astype(BF16)
    logit = _dot(hi, wr_hi_ref[...]) + _dot(lo, wr_hi_ref[...]) + _dot(hi, wr_lo_ref[...]) + br_ref[...]
    lane = lax.broadcasted_iota(jnp.int32, (tm, LANES), 1).astype(F32)
    far = float(LANES)

    def first_argmax(vals, valid):
        mx = jnp.max(jnp.where(valid, vals, NEG), axis=-1, keepdims=True)
        idx = jnp.min(jnp.where(valid & (vals == mx), lane, far), axis=-1, keepdims=True)
        return mx, idx

    gmask = lane < N_GROUPS
    gmax, gidx = first_argmax(logit, gmask)
    pg_top = 1.0 / jnp.sum(jnp.where(gmask, jnp.exp(logit - gmax), 0.0), axis=-1, keepdims=True)
    e_lo = N_GROUPS + EXPERTS_PER_GROUP * gidx
    emask = (lane >= e_lo) & (lane < e_lo + EXPERTS_PER_GROUP)
    m1, i1 = first_argmax(logit, emask)
    m2, i2 = first_argmax(logit, emask & (lane != i1))
    e = jnp.exp(m2 - m1)
    g1 = pg_top / (1.0 + e)
    g2 = pg_top * e / (1.0 + e)
    route_ref[...] = jnp.where(lane == 0, i1 - N_GROUPS, jnp.where(lane == 1, i2 - N_GROUPS,
                               jnp.where(lane == 2, g1, jnp.where(lane == 3, g2, 0.0))))


def _outproj(x, o, y, ow, *, tm, att_padded):
    b, t, d = x.shape
    c = y.shape[-1] // 2
    d_att = 512
    row = lambda n: pl.BlockSpec((None, tm, n), lambda i, j: (i, j, 0))
    const = lambda a: pl.BlockSpec(a.shape, lambda i, j: (0,) * a.ndim)
    if att_padded:
        o_spec = pl.BlockSpec((None, o.shape[1], tm, LANES), lambda i, j: (i, 0, j, 0))
    else:
        o_spec = row(d_att)
    kern = functools.partial(_outproj_kernel, att_padded=att_padded, d_att=d_att, c=c)
    return pl.pallas_call(
        kern, grid=(b, t // tm), name="outproj_heads" if att_padded else "outproj_flat",
        in_specs=[row(d), o_spec, row(2 * c)] + [const(a) for a in ow],
        out_specs=[row(d), row(d), row(LANES)],
        out_shape=[jax.ShapeDtypeStruct((b, t, d), F32), jax.ShapeDtypeStruct((b, t, d), F32),
                   jax.ShapeDtypeStruct((b, t, LANES), F32)],
        compiler_params=_params("parallel", "parallel"),
    )(x, o, y, *ow)


def _rank_kernel(route_ref, cin_ref, tri_ref, rank_ref, cnt_ref, carry_ref):
    @pl.when(pl.program_id(0) == 0)
    def _():
        carry_ref[...] = cin_ref[...]

    route = route_ref[...]
    tm = route.shape[0]
    lane = lax.broadcasted_iota(jnp.int32, (tm, LANES), 1).astype(F32)
    is1 = lane == route[:, 0:1]
    is2 = lane == route[:, 1:2]
    onehot = jnp.where(is1, 1.0, jnp.where(is2, 1.0, 0.0))
    before = _dot(tri_ref[...], onehot.astype(BF16)) + carry_ref[...]
    r1 = jnp.sum(jnp.where(is1, before, 0.0), axis=-1, keepdims=True)
    r2 = jnp.sum(jnp.where(is2, before, 0.0), axis=-1, keepdims=True)
    rank_ref[...] = jnp.where(lane == 0, r1, jnp.where(lane == 1, r2, 0.0))
    carry_ref[...] = carry_ref[...] + jnp.sum(onehot, axis=0, keepdims=True)
    cnt_ref[...] = carry_ref[...]


def _rank(route, counts_in, *, tm):
    n = route.shape[0]
    tri = (jnp.arange(tm)[:, None] > jnp.arange(tm)[None, :]).astype(BF16)
    return pl.pallas_call(
        _rank_kernel, grid=(n // tm,), name="rank",
        in_specs=[pl.BlockSpec((tm, LANES), lambda i: (i, 0)), pl.BlockSpec((1, LANES), lambda i: (0, 0)),
                  pl.BlockSpec((tm, tm), lambda i: (0, 0))],
        out_specs=[pl.BlockSpec((tm, LANES), lambda i: (i, 0)), pl.BlockSpec((1, LANES), lambda i: (0, 0))],
        out_shape=[jax.ShapeDtypeStruct((n, LANES), F32), jax.ShapeDtypeStruct((1, LANES), F32)],
        scratch_shapes=[pltpu.VMEM((1, LANES), F32)],
        compiler_params=_params("arbitrary"),
    )(route, counts_in, tri)


def _dispatch(e_all, rank_all, counts, n_blocks):
    blk = MOE_BLOCK
    nk = e_all.shape[0] * TOP_K
    nblk_e = (counts + blk - 1) // blk
    bend = jnp.cumsum(nblk_e)
    bstart = bend - nblk_e
    onehot = e_all[:, :, None] == jnp.arange(N_EXPERTS, dtype=jnp.int32)[None, None, :]
    dest = rank_all + jnp.sum(jnp.where(onehot, (bstart * blk)[None, None, :], 0), axis=-1)
    src = jnp.zeros((n_blocks * blk,), jnp.int32).at[dest.reshape(nk)].set(jnp.arange(nk, dtype=jnp.int32))
    n_used = bend[-1].astype(jnp.int32)
    bidx = jnp.arange(n_blocks, dtype=jnp.int32)
    blk_e = jnp.minimum(jnp.sum((bend[None, :] <= bidx[:, None]).astype(jnp.int32), axis=1), N_EXPERTS - 1)
    eq = blk_e[:, None] == jnp.arange(N_EXPERTS, dtype=jnp.int32)[None, :]
    cnt_b = jnp.sum(jnp.where(eq, counts[None, :], 0), axis=1)
    start_b = jnp.sum(jnp.where(eq, bstart[None, :], 0), axis=1)
    n_valid = jnp.clip(cnt_b - (bidx - start_b) * blk, 0, blk)
    n_valid = jnp.where(bidx < n_used, n_valid, 0).astype(jnp.int32)
    last_e = jnp.sum(jnp.where(bidx == n_used - 1, blk_e, 0))
    blk_e = jnp.where(bidx < n_used, blk_e, last_e).astype(jnp.int32)
    return blk_e, n_valid, n_used.reshape(1), src


def _for_rows(n, fn):
    n_main = n // DMA_UNROLL

    def main(t, _):
        for u in range(DMA_UNROLL):
            fn(t * DMA_UNROLL + u)
        return 0

    def rest(r, _):
        fn(r)
        return 0

    lax.fori_loop(0, n_main, main, 0)
    lax.fori_loop(n_main * DMA_UNROLL, n, rest, 0)


def _moe_kernel(be_ref, nv_ref, nu_ref, src_ref, xn_hbm, wg_ref, wu_ref, wd_ref, y_hbm,
                xbuf, ybuf, gsem, ssem, wgb, wub, wdb, *, blk):
    i = pl.program_id(0)
    n_used = nu_ref[0]
    slot = i % 2

    def gather_copy(bi, sl, r):
        tok = lax.shift_right_logical(src_ref[bi * blk + r], TOP_K.bit_length() - 1)
        return pltpu.make_async_copy(xn_hbm.at[pl.ds(tok, 1)], xbuf.at[sl, pl.ds(r, 1)], gsem.at[sl])

    def scatter_copy(bi, sl, r):
        slot_row = src_ref[bi * blk + r]
        return pltpu.make_async_copy(ybuf.at[sl, pl.ds(r, 1)], y_hbm.at[pl.ds(slot_row, 1)], ssem.at[sl])

    def start_all(make, bi, sl):
        _for_rows(nv_ref[bi], lambda r: make(bi, sl, r).start())

    def wait_all(make, bi, sl):
        _for_rows(nv_ref[bi], lambda r: make(bi, sl, 0).wait())

    @pl.when(i == 0)
    def _():
        xbuf[...] = jnp.zeros_like(xbuf)

        @pl.when(n_used > 0)
        def _():
            start_all(gather_copy, 0, 0)

    @pl.when(i + 1 < n_used)
    def _():
        start_all(gather_copy, i + 1, 1 - slot)

    @pl.when(i < n_used)
    def _():
        @pl.when((i == 0) | (be_ref[i] != be_ref[jnp.maximum(i - 1, 0)]))
        def _():
            wgb[...] = wg_ref[...].astype(BF16)
            wub[...] = wu_ref[...].astype(BF16)
            wdb[...] = wd_ref[...].astype(BF16)

        wait_all(gather_copy, i, slot)
        x = xbuf[slot].astype(BF16)
        g = _dot(x, wgb[...])
        u = _dot(x, wub[...])
        hmid = (g * _sigmoid(g) * u).astype(BF16)
        y = _dot(hmid, wdb[...])

        @pl.when(i >= 2)
        def _():
            wait_all(scatter_copy, i - 2, slot)

        ybuf[slot] = y
        start_all(scatter_copy, i, slot)

        @pl.when(i == n_used - 1)
        def _():
            @pl.when(i >= 1)
            def _():
                wait_all(scatter_copy, i - 1, 1 - slot)

            wait_all(scatter_copy, i, slot)


def _moe_experts(blk_e, n_valid, n_used, src, xn_all, w_gate, w_up, w_down, *, layer, n_blocks):
    n_all, d = xn_all.shape
    de = w_gate.shape[-1]
    blk = MOE_BLOCK
    wspec = lambda r, c_: pl.BlockSpec((None, None, r, c_), lambda i, be, nv, nu, s: (layer, be[i], 0, 0))
    grid_spec = pltpu.PrefetchScalarGridSpec(
        num_scalar_prefetch=4, grid=(n_blocks,),
        in_specs=[pl.BlockSpec(memory_space=pl.ANY), wspec(d, de), wspec(d, de), wspec(de, d)],
        out_specs=pl.BlockSpec(memory_space=pl.ANY),
        scratch_shapes=[pltpu.VMEM((2, blk, d), F32), pltpu.VMEM((2, blk, d), F32),
                        pltpu.SemaphoreType.DMA((2,)), pltpu.SemaphoreType.DMA((2,)),
                        pltpu.VMEM((d, de), BF16), pltpu.VMEM((d, de), BF16), pltpu.VMEM((de, d), BF16)])
    return pl.pallas_call(
        functools.partial(_moe_kernel, blk=blk), grid_spec=grid_spec, name="moe_experts",
        out_shape=jax.ShapeDtypeStruct((n_all * TOP_K, d), F32),
        compiler_params=_params("arbitrary"),
    )(blk_e, n_valid, n_used, src, xn_all, w_gate, w_up, w_down)


def _final_kernel(x2_ref, y_ref, route_ref, g_ref, o_ref):
    o_ref[...] = _rms(_moe_combine(x2_ref[...], route_ref[...], y_ref[...]), g_ref[...])


def _final(x2, y, route, g, *, tm, y_block_offset):
    n, d = x2.shape
    return pl.pallas_call(
        _final_kernel, grid=(n // tm,), name="final",
        in_specs=[pl.BlockSpec((tm, d), lambda i: (i, 0)),
                  pl.BlockSpec((tm, 2 * d), lambda i: (y_block_offset + i, 0)),
                  pl.BlockSpec((tm, LANES), lambda i: (i, 0)), pl.BlockSpec((1, d), lambda i: (0, 0))],
        out_specs=pl.BlockSpec((tm, d), lambda i: (i, 0)),
        out_shape=jax.ShapeDtypeStruct((n, d), F32),
        compiler_params=_params("parallel"),
    )(x2, y, route, g)


def _block_diag(w):
    n, a, b = w.shape
    out = jnp.zeros((n * a, n * b), w.dtype)
    for i in range(n):
        out = out.at[i * a:(i + 1) * a, i * b:(i + 1) * b].set(w[i])
    return out


def _pad_lanes(v, n=LANES):
    return jnp.zeros((1, n), F32).at[0, :v.shape[0]].set(v)


def _tile_for(t, pref):
    tm = min(pref, t)
    while t % tm:
        tm //= 2
    return tm


def kernel(x_prompt, x_sample, cache_k, cache_v, cache_logf, state_conv, state_rglru, state_pool, page_table, norm_mix, w_in, b_forget, conv_w, conv_b, w_gate_a, b_gate_a, w_gate_x, b_gate_x, lru_lambda, w_pool, pool_scale, norm_groups, w_out, norm_ffn, w_router_group, b_router_group, w_router_expert, b_router_expert, w_up, w_gate, w_down, norm_final):
    depth = w_in.shape[0]
    b_p, t_p, d = x_prompt.shape
    b_s = x_sample.shape[0]
    n_heads = b_forget.shape[1]
    d_att = n_heads * HEAD_DIM
    c = conv_w.shape[-1]
    n_p = b_p * t_p
    n_all = n_p + b_s
    n_pages, page = page_table.shape[1], cache_k.shape[2]
    past_len = n_pages * page
    nk = n_all * TOP_K
    n_blocks = -(-(nk + N_EXPERTS * (MOE_BLOCK - 1)) // MOE_BLOCK)

    tm_p = _tile_for(t_p, 512)
    tk = _tile_for(t_p, 512)
    n_chains = 2 if t_p % (2 * tk) == 0 else 1
    tr = _tile_for(t_p, 256)
    tm_r = _tile_for(n_p, 1024)
    tm_f = _tile_for(n_p, 512)

    cache_kt = cache_k.transpose(0, 1, 3, 4, 2)
    cache_vt = cache_v.transpose(0, 1, 3, 4, 2)
    x_p = x_prompt
    x_s = x_sample.reshape(1, b_s, d)
    moe_p = moe_s = None
    off_f = 3 * d_att
    outs = [[] for _ in range(12)]
    for l in range(depth):
        wl = w_in[l]
        w_cat = jnp.concatenate(
            [wl[:, :off_f], wl[:, off_f + n_heads:], wl[:, off_f:off_f + n_heads],
             jnp.zeros((d, LANES - n_heads), F32)], axis=1).astype(BF16)
        g_mix = norm_mix[l].reshape(1, d)
        b_pad = _pad_lanes(b_forget[l])
        mw = (conv_w[l], conv_b[l].reshape(1, c), _block_diag(w_gate_a[l]).astype(BF16),
              b_gate_a[l].reshape(1, c), _block_diag(w_gate_x[l]).astype(BF16), b_gate_x[l].reshape(1, c),
              lru_lambda[l].reshape(1, c), _block_diag(w_pool[l]).astype(BF16), pool_scale[l].reshape(1, c))
        ng, wo = norm_groups[l], w_out[l]
        ga_pad = jnp.zeros((n_heads, 1, LANES), F32).at[:, 0, :HEAD_DIM].set(ng[:d_att].reshape(n_heads, HEAD_DIM))
        woa_pad = jnp.zeros((n_heads, LANES, d), F32).at[:, :HEAD_DIM].set(
            wo[:d_att].reshape(n_heads, HEAD_DIM, d)).astype(BF16)
        w_r = jnp.zeros((d, LANES), F32).at[:, :N_GROUPS].set(w_router_group[l]).at[
            :, N_GROUPS:N_GROUPS + N_EXPERTS].set(w_router_expert[l])
        wr_hi = w_r.astype(BF16)
        wr_lo = (w_r - wr_hi.astype(F32)).astype(BF16)
        b_r = _pad_lanes(jnp.concatenate([b_router_group[l], b_router_expert[l]]))
        ow_rest = (wo[d_att:].astype(BF16), norm_ffn[l].reshape(1, d), wr_hi, wr_lo, b_r)
        g_rest = ng[d_att:].reshape(1, 2 * c)
        ow_p = (ga_pad, g_rest, woa_pad, *ow_rest)
        ow_s = (ng[:d_att].reshape(1, d_att), g_rest, wo[:d_att].astype(BF16), *ow_rest)

        res = _inproj(x_p, g_mix, w_cat, tm=tm_p, head_layout=True, moe=moe_p, b_pad=b_pad)
        if moe_p is not None:
            x_p, res = res[0], res[1:]
        qa, ka, vp, k_p, v_p, lf_p, rest_p = res
        o_p = _attention(qa, ka, vp, tk=tk, n_chains=n_chains)
        y_mix_p, conv_t, h_t, pool_t = _mix_prompt(rest_p, mw, tr=tr)
        x2_p, xn2_p, route_p = _outproj(x_p, o_p, y_mix_p, ow_p, tm=tm_p, att_padded=True)

        res = _inproj(x_s, g_mix, w_cat, tm=b_s, head_layout=False, moe=moe_s, y_block_offset=n_p // b_s)
        if moe_s is not None:
            x_s, res = res[0], res[1:]
        q_s, k_s, v_s, f_s, rest_s = res
        heads = lambda a: a.reshape(b_s, n_heads, HEAD_DIM)
        lfp_t = cache_logf[l][page_table].reshape(b_s, past_len, n_heads).transpose(0, 2, 1)
        o_s, lf_s = _decode_attention(page_table, heads(q_s), heads(k_s), heads(v_s), f_s[0], b_pad, lfp_t,
                                      cache_kt, cache_vt, layer=l)
        y_mix_s, conv_s, h_s, pool_s = _mix_sample(
            rest_s[0], state_conv[l].transpose(1, 0, 2), state_rglru[l], state_pool[l].transpose(1, 0, 2),
            mw, pos0=past_len)
        x2_s, xn2_s, route_s = _outproj(x_s, o_s.reshape(1, b_s, d_att), y_mix_s[None], ow_s, tm=b_s,
                                        att_padded=False)

        route_p2, route_s2 = route_p.reshape(n_p, LANES), route_s.reshape(b_s, LANES)
        rank_p, cnt_p = _rank(route_p2, jnp.zeros((1, LANES), F32), tm=tm_r)
        rank_s, cnt = _rank(route_s2, cnt_p, tm=b_s)
        first2 = lambda a_p, a_s: jnp.concatenate([a_p[:, :TOP_K], a_s[:, :TOP_K]], axis=0).astype(jnp.int32)
        blk_e, n_valid, n_used, src = _dispatch(first2(route_p2, route_s2), first2(rank_p, rank_s),
                                                cnt[0, :N_EXPERTS].astype(jnp.int32), n_blocks)
        xn2_all = jnp.concatenate([xn2_p.reshape(n_p, d), xn2_s.reshape(b_s, d)], axis=0)
        y_moe = _moe_experts(blk_e, n_valid, n_used, src, xn2_all, w_gate, w_up, w_down,
                             layer=l, n_blocks=n_blocks).reshape(n_all, TOP_K * d)
        x_p, x_s = x2_p, x2_s
        moe_p, moe_s = (y_moe, route_p), (y_moe, route_s)

        per_layer = (
            k_p.reshape(b_p, t_p, n_heads, HEAD_DIM), v_p.reshape(b_p, t_p, n_heads, HEAD_DIM),
            lf_p[:, :, :n_heads], conv_t[:, 8 - (CONV_W - 1):], h_t[:, 7], pool_t[:, 16 - POOL_BUF:],
            k_s.reshape(b_s, 1, n_heads, HEAD_DIM), v_s.reshape(b_s, 1, n_heads, HEAD_DIM),
            lf_s[:, None, :n_heads], conv_s.transpose(1, 0, 2), h_s, pool_s.transpose(1, 0, 2))
        for acc, val in zip(outs, per_layer):
            acc.append(val)
    g_fin = norm_final.reshape(1, d)
    y_prompt = _final(x_p.reshape(n_p, d), moe_p[0], moe_p[1].reshape(n_p, LANES), g_fin,
                      tm=tm_f, y_block_offset=0).reshape(b_p, t_p, d)
    y_sample = _final(x_s.reshape(b_s, d), moe_s[0], moe_s[1].reshape(b_s, LANES), g_fin,
                      tm=b_s, y_block_offset=n_p // b_s).reshape(b_s, 1, d)
    return (y_prompt, y_sample, *[jnp.stack(o) for o in outs])
```

```python
import functools

import jax
import jax.numpy as jnp
from jax import lax
from jax.experimental import pallas as pl
from jax.experimental.pallas import tpu as pltpu

F32 = jnp.float32
BF16 = jnp.bfloat16

EPS = 1e-6
HEAD_DIM = 64
LANES = 128
RG_C = 8.0
CONV_W = 4
POOL_WINDOWS = (2, 4, 8, 16)
POOL_BUF = max(POOL_WINDOWS) - 1
N_GROUPS = 4
EXPERTS_PER_GROUP = 8
N_EXPERTS = N_GROUPS * EXPERTS_PER_GROUP
TOP_K = 2
MOE_BLOCK = 256
DMA_UNROLL = 8
NEG = -1e30
LOG2E = 1.4426950408889634
N_BIAS = 3
VMEM_LIMIT = 56 * 1024 * 1024


def _params(*sem):
    return pltpu.CompilerParams(dimension_semantics=sem, vmem_limit_bytes=VMEM_LIMIT)


def _split3(x):
    hi = x.astype(BF16)
    r1 = x - hi.astype(F32)
    mid = r1.astype(BF16)
    lo = (r1 - mid.astype(F32)).astype(BF16)
    return hi, mid, lo


def _dot(a, b):
    return jnp.dot(a, b, preferred_element_type=F32)


def _dot_nt(a, b):
    return lax.dot_general(a, b, (((1,), (1,)), ((), ())), preferred_element_type=F32)


def _dot3(x, w_bf16):
    hi, mid, lo = _split3(x)
    return _dot(hi, w_bf16) + _dot(mid, w_bf16) + _dot(lo, w_bf16)


def _dot3_left(w_bf16, x):
    hi, mid, lo = _split3(x)
    return _dot(w_bf16, hi) + _dot(w_bf16, mid) + _dot(w_bf16, lo)


def _sigmoid(x):
    return 1.0 / (1.0 + jnp.exp(-x))


def _softplus(x):
    return jnp.maximum(x, 0.0) + jnp.log1p(jnp.exp(-jnp.abs(x)))


def _gelu_tanh(x):
    return 0.5 * x * (1.0 + jnp.tanh(0.7978845608028654 * (x + 0.044715 * (x * x * x))))


def _rms(x, g):
    return x * lax.rsqrt(jnp.mean(x * x, axis=-1, keepdims=True) + EPS) * g


def _moe_combine(x2, route, y_ref):
    tm, d = x2.shape
    n_chunks = d // LANES
    g1, g2 = route[:, 2:3], route[:, 3:4]
    step = TOP_K * n_chunks
    out = []
    for c in range(n_chunks):
        y0 = y_ref[pl.ds(c, tm, stride=step), :]
        y1 = y_ref[pl.ds(n_chunks + c, tm, stride=step), :]
        out.append(x2[:, c * LANES:(c + 1) * LANES] + (g1 * y0 + g2 * y1))
    return jnp.concatenate(out, axis=1)


def _inproj_kernel(*refs, d_att, d_rest, head_layout, has_moe, scale):
    refs = list(refs)
    x_ref = refs.pop(0)
    if has_moe:
        y_ref, route_ref = refs.pop(0), refs.pop(0)
    g_ref, w_ref = refs.pop(0), refs.pop(0)
    if head_layout:
        b_ref, tri_ref, place_ref = refs.pop(0), refs.pop(0), refs.pop(0)
    x = x_ref[...]
    if has_moe:
        x = _moe_combine(x, route_ref[...], y_ref)
        refs.pop(0)[...] = x
    xn = _rms(x, g_ref[...]).astype(BF16)

    def sec(lo, n):
        return _dot(xn, w_ref[:, lo:lo + n])

    q = sec(0, d_att) * scale
    k = sec(d_att, d_att)
    v = sec(2 * d_att, d_att)
    rest = sec(3 * d_att, d_rest)
    f = sec(3 * d_att + d_rest, LANES)
    if not head_layout:
        q_ref, k_ref, v_ref, f_ref, r_ref = refs
        q_ref[...] = q
        k_ref[...] = k
        v_ref[...] = v
        f_ref[...] = f
        r_ref[...] = rest
        return
    qa_ref, ka_ref, vp_ref, k_ref, v_ref, lf_ref, r_ref, carry_ref = refs
    k_ref[...] = k
    v_ref[...] = v
    r_ref[...] = rest

    @pl.when(pl.program_id(1) == 0)
    def _():
        carry_ref[...] = jnp.zeros_like(carry_ref)

    lf = -_softplus(-(f + b_ref[...]))
    lf_ref[...] = lf
    cs = _dot3_left(tri_ref[...], lf) + carry_ref[...]
    carry_ref[...] = carry_ref[...] + jnp.sum(lf, axis=0, keepdims=True)

    tm = q.shape[0]
    n_heads = d_att // HEAD_DIM
    lane = lax.broadcasted_iota(jnp.int32, (tm, LANES), 1)
    parts = [p.astype(F32) for p in _split3(cs * (-LOG2E))]
    packed = jnp.where(lane < n_heads, parts[0], 0.0)
    for j in range(1, N_BIAS):
        in_j = (lane >= j * n_heads) & (lane < (j + 1) * n_heads)
        packed = packed + jnp.where(in_j, pltpu.roll(parts[j], j * n_heads, axis=1), 0.0)
    aug = _dot(packed.astype(BF16), place_ref[...])

    low = lane < HEAD_DIM
    q_fill = jnp.where((lane >= HEAD_DIM) & (lane < HEAD_DIM + N_BIAS), 1.0, 0.0)
    v_fill = jnp.where(lane == HEAD_DIM, 1.0, 0.0)
    for j in range(d_att // LANES):
        sl = slice(j * LANES, (j + 1) * LANES)
        for val, ref, fills in ((q, qa_ref, (q_fill, q_fill)), (v, vp_ref, (v_fill, v_fill)),
                                (k, ka_ref, (aug[:, 2 * j * LANES:(2 * j + 1) * LANES],
                                             aug[:, (2 * j + 1) * LANES:(2 * j + 2) * LANES]))):
            pair = val[:, sl]
            ref[2 * j] = jnp.where(low, pair, fills[0]).astype(BF16)
            ref[2 * j + 1] = jnp.where(low, pltpu.roll(pair, HEAD_DIM, axis=1), fills[1]).astype(BF16)


def _bias_placement(n_heads):
    rows = jnp.arange(LANES)[:, None]
    cols = jnp.arange(n_heads * LANES)[None, :]
    j, h = rows // n_heads, rows % n_heads
    hit = (rows < N_BIAS * n_heads) & (cols == h * LANES + HEAD_DIM + j)
    return hit.astype(BF16)


def _inproj(x, g, w, *, tm, head_layout, moe=None, b_pad=None, y_block_offset=0):
    b, t, d = x.shape
    n_w = w.shape[1]
    d_att = 512
    d_rest = n_w - 3 * d_att - LANES
    n_heads = d_att // HEAD_DIM
    nt = t // tm
    row = lambda n: pl.BlockSpec((None, tm, n), lambda i, j: (i, j, 0))
    const = lambda a: pl.BlockSpec(a.shape, lambda i, j: (0,) * a.ndim)
    slab = pl.BlockSpec((None, n_heads, tm, LANES), lambda i, j: (i, 0, j, 0))
    f32s = lambda n: jax.ShapeDtypeStruct((b, t, n), F32)
    args, in_specs = [x], [row(d)]
    out_shape, out_specs = [], []
    if moe is not None:
        y, route = moe
        args += [y, route]
        in_specs += [pl.BlockSpec((tm * TOP_K * d // LANES, LANES), lambda i, j: (y_block_offset + i * nt + j, 0)),
                     row(LANES)]
        out_shape.append(f32s(d))
        out_specs.append(row(d))
    args += [g, w]
    in_specs += [const(g), const(w)]
    scratch = []
    if head_layout:
        tri = (jnp.arange(tm)[:, None] >= jnp.arange(tm)[None, :]).astype(BF16)
        place = _bias_placement(n_heads)
        args += [b_pad, tri, place]
        in_specs += [const(b_pad), const(tri), const(place)]
        slab_s = jax.ShapeDtypeStruct((b, n_heads, t, LANES), BF16)
        out_shape += [slab_s, slab_s, slab_s]
        out_specs += [slab, slab, slab]
        scratch = [pltpu.VMEM((1, LANES), F32)]
        scale = HEAD_DIM ** -0.5 * LOG2E
    else:
        out_shape.append(f32s(d_att))
        out_specs.append(row(d_att))
        scale = HEAD_DIM ** -0.5
    out_shape += [f32s(d_att), f32s(d_att), f32s(LANES), f32s(d_rest)]
    out_specs += [row(d_att), row(d_att), row(LANES), row(d_rest)]
    kern = functools.partial(_inproj_kernel, d_att=d_att, d_rest=d_rest, head_layout=head_layout,
                             has_moe=moe is not None, scale=scale)
    return pl.pallas_call(
        kern, grid=(b, nt), name="inproj_heads" if head_layout else "inproj_flat",
        in_specs=in_specs, out_specs=out_specs, out_shape=out_shape, scratch_shapes=scratch,
        compiler_params=_params("parallel", "arbitrary"),
    )(*args)


def _attention_kernel(q_ref, k_ref, v_ref, o_ref, *, tk, n_chains):
    qi = pl.program_id(2)
    qs = [q_ref[c * tk:(c + 1) * tk, :] for c in range(n_chains)]

    def kv(j):
        off = pl.multiple_of(j * tk, tk)
        return k_ref[pl.ds(off, tk), :], v_ref[pl.ds(off, tk), :]

    def tile(q, k, v, m, acc, masked):
        s = _dot_nt(q, k)
        if masked:
            row = lax.broadcasted_iota(jnp.int32, (tk, tk), 0)
            col = lax.broadcasted_iota(jnp.int32, (tk, tk), 1)
            s = jnp.where(row >= col, s, NEG)
        m_new = jnp.maximum(m, jnp.max(s, axis=-1, keepdims=True))
        p = jnp.exp2(s - m_new)
        acc = jnp.exp2(m - m_new) * acc + _dot(p.astype(BF16), v)
        return m_new, acc

    def body(j, carry):
        for d in range(n_chains):
            k, v = kv(j * n_chains + d)
            carry = tuple(tile(qs[c], k, v, *carry[c], False) for c in range(n_chains))
        return carry

    init = tuple((jnp.full((tk, 1), NEG, F32), jnp.zeros((tk, LANES), F32)) for _ in range(n_chains))
    carry = list(lax.fori_loop(0, qi, body, init))
    for d in range(n_chains):
        k, v = kv(qi * n_chains + d)
        for c in range(d, n_chains):
            carry[c] = tile(qs[c], k, v, *carry[c], c == d)
    for c in range(n_chains):
        acc = carry[c][1]
        o_ref[c * tk:(c + 1) * tk, :] = acc / acc[:, HEAD_DIM:HEAD_DIM + 1]


def _attention(qa, ka, vp, *, tk, n_chains):
    b, h, t, _ = qa.shape
    tq = tk * n_chains
    tile = pl.BlockSpec((None, None, tq, LANES), lambda i, j, n: (i, j, n, 0))
    full = pl.BlockSpec((None, None, t, LANES), lambda i, j, n: (i, j, 0, 0))
    return pl.pallas_call(
        functools.partial(_attention_kernel, tk=tk, n_chains=n_chains), grid=(b, h, t // tq), name="attention",
        in_specs=[tile, full, full], out_specs=tile,
        out_shape=jax.ShapeDtypeStruct((b, h, t, LANES), F32),
        compiler_params=_params("parallel", "parallel", "arbitrary"),
    )(qa, ka, vp)


def _rglru_gates(xc, gb, wa_ref, ba_ref, wx_ref, bx_ref, lam_ref):
    xcb = xc.astype(BF16)
    r = _sigmoid(_dot(xcb, wa_ref[...]) + ba_ref[...])
    i = _sigmoid(_dot(xcb, wx_ref[...]) + bx_ref[...])
    log_a = -RG_C * r * _softplus(-lam_ref[...])
    a = jnp.exp(log_a)
    u = jnp.sqrt(jnp.tanh(-log_a) * (a * a + 1.0)) * (i * xc)
    return a, u, _gelu_tanh(gb)


def _pool_lane_select(c, w2, w4, w8, w16):
    lane = lax.broadcasted_iota(jnp.int32, c.shape, 1)
    gw = c.shape[1] // len(POOL_WINDOWS)
    return jnp.where(lane < gw, w2, jnp.where(lane < 2 * gw, w4, jnp.where(lane < 3 * gw, w8, w16)))


def _mix_prompt_kernel(r_ref, cw_ref, cb_ref, wa_ref, ba_ref, wx_ref, bx_ref, lam_ref, wp_ref, ps_ref,
                       y_ref, conv_ref, h_ref, pool_ref, xprev_ref, uprev_ref, htail_ref, *, tr, c):
    t = pl.program_id(1)

    @pl.when(t == 0)
    def _():
        xprev_ref[...] = jnp.zeros_like(xprev_ref)
        uprev_ref[...] = jnp.zeros_like(uprev_ref)
        htail_ref[...] = jnp.zeros_like(htail_ref)

    xb = r_ref[:, 0:c]
    gb = r_ref[:, c:2 * c]
    u_in = r_ref[:, 2 * c:3 * c]

    ext = jnp.concatenate([xprev_ref[...], xb], axis=0)
    xc = cb_ref[...] + cw_ref[CONV_W - 1:CONV_W, :] * xb
    for j in range(1, CONV_W):
        xc = xc + cw_ref[CONV_W - 1 - j:CONV_W - j, :] * pltpu.roll(ext, j, axis=0)[8:]

    a, u, gate = _rglru_gates(xc, gb, wa_ref, ba_ref, wx_ref, bx_ref, lam_ref)

    row = lax.broadcasted_iota(jnp.int32, (tr, c), 0)
    big_a, big_u = a, u
    sh = 1
    while sh < tr:
        keep = row >= sh
        a_s = jnp.where(keep, pltpu.roll(big_a, sh, axis=0), 1.0)
        u_s = jnp.where(keep, pltpu.roll(big_u, sh, axis=0), 0.0)
        big_u = big_a * u_s + big_u
        big_a = big_a * a_s
        sh *= 2
    h = big_u + big_a * htail_ref[7:8, :]

    extu = jnp.concatenate([uprev_ref[...], u_in], axis=0)
    s2 = extu + pltpu.roll(extu, 1, axis=0)
    s4 = s2 + pltpu.roll(s2, 2, axis=0)
    s8 = s4 + pltpu.roll(s4, 4, axis=0)
    s16 = s8 + pltpu.roll(s8, 8, axis=0)
    win = _pool_lane_select(u_in, s2[16:], s4[16:], s8[16:], s16[16:])
    wsz = _pool_lane_select(u_in, 2.0, 4.0, 8.0, 16.0)
    pos = (t * tr + row).astype(F32)
    d = win / jnp.minimum(pos + 1.0, wsz) - u_in
    y_pool = _dot(d.astype(BF16), wp_ref[...]) * ps_ref[...]

    y_ref[:, 0:c] = h * gate
    y_ref[:, c:2 * c] = y_pool
    conv_ref[...] = xb[tr - 8:]
    h_ref[...] = h[tr - 8:]
    pool_ref[...] = u_in[tr - 16:]
    xprev_ref[...] = xb[tr - 8:]
    uprev_ref[...] = u_in[tr - 16:]
    htail_ref[...] = h[tr - 8:]


def _mix_prompt(rest, mw, *, tr):
    b, t, c3 = rest.shape
    c = c3 // 3
    vec = lambda n: pl.BlockSpec((n, c), lambda i, j: (0, 0))
    mat = pl.BlockSpec((c, c), lambda i, j: (0, 0))
    tail = lambda n: pl.BlockSpec((None, n, c), lambda i, j: (i, 0, 0))
    return pl.pallas_call(
        functools.partial(_mix_prompt_kernel, tr=tr, c=c), grid=(b, t // tr), name="mix_prompt",
        in_specs=[pl.BlockSpec((None, tr, c3), lambda i, j: (i, j, 0)),
                  vec(CONV_W), vec(1), mat, vec(1), mat, vec(1), vec(1), mat, vec(1)],
        out_specs=[pl.BlockSpec((None, tr, 2 * c), lambda i, j: (i, j, 0)), tail(8), tail(8), tail(16)],
        out_shape=[jax.ShapeDtypeStruct((b, t, 2 * c), F32), jax.ShapeDtypeStruct((b, 8, c), F32),
                   jax.ShapeDtypeStruct((b, 8, c), F32), jax.ShapeDtypeStruct((b, 16, c), F32)],
        scratch_shapes=[pltpu.VMEM((8, c), F32), pltpu.VMEM((16, c), F32), pltpu.VMEM((8, c), F32)],
        compiler_params=_params("parallel", "arbitrary"),
    )(rest, *mw)


def _mix_sample_kernel(r_ref, sc_ref, h0_ref, sp_ref, cw_ref, cb_ref, wa_ref, ba_ref, wx_ref, bx_ref,
                       lam_ref, wp_ref, ps_ref, y_ref, conv_ref, h_ref, pool_ref, *, c, pos0):
    xb = r_ref[:, 0:c]
    gb = r_ref[:, c:2 * c]
    u_in = r_ref[:, 2 * c:3 * c]

    xc = cb_ref[...] + cw_ref[CONV_W - 1:CONV_W, :] * xb
    for j in range(CONV_W - 1):
        xc = xc + cw_ref[j:j + 1, :] * sc_ref[j]
    a, u, gate = _rglru_gates(xc, gb, wa_ref, ba_ref, wx_ref, bx_ref, lam_ref)
    h = a * h0_ref[...] + u

    sums = {}
    acc = u_in
    for n in range(1, POOL_BUF + 1):
        acc = acc + sp_ref[POOL_BUF - n]
        if n + 1 in POOL_WINDOWS:
            sums[n + 1] = acc
    win = _pool_lane_select(u_in, *[sums[w] for w in POOL_WINDOWS])
    cnt = _pool_lane_select(u_in, *[float(min(pos0 + 1, w)) for w in POOL_WINDOWS])
    d = win / cnt - u_in
    y_pool = _dot(d.astype(BF16), wp_ref[...]) * ps_ref[...]

    y_ref[:, 0:c] = h * gate
    y_ref[:, c:2 * c] = y_pool
    h_ref[...] = h
    for j in range(CONV_W - 2):
        conv_ref[j] = sc_ref[j + 1]
    conv_ref[CONV_W - 2] = xb
    for j in range(POOL_BUF - 1):
        pool_ref[j] = sp_ref[j + 1]
    pool_ref[POOL_BUF - 1] = u_in


def _mix_sample(rest, sc_t, h0, sp_t, mw, *, pos0):
    bs, c3 = rest.shape
    c = c3 // 3
    return pl.pallas_call(
        functools.partial(_mix_sample_kernel, c=c, pos0=pos0), name="mix_sample",
        out_shape=[jax.ShapeDtypeStruct((bs, 2 * c), F32), jax.ShapeDtypeStruct(sc_t.shape, F32),
                   jax.ShapeDtypeStruct((bs, c), F32), jax.ShapeDtypeStruct(sp_t.shape, F32)],
        compiler_params=pltpu.CompilerParams(vmem_limit_bytes=VMEM_LIMIT),
    )(rest, sc_t, h0, sp_t, *mw)


def _decode_kernel(pt_ref, q_ref, kn_ref, vn_ref, f_ref, b_ref, lfp_ref, tri_ref, ck_hbm, cv_hbm,
                   o_ref, lfn_ref, kbuf, vbuf, ksem, vsem, *, layer, n_pages, page, n_heads):
    b = pl.program_id(0)
    nb = pl.num_programs(0)

    def copies(seq, slot):
        out = []
        for j in range(n_pages):
            pid = pt_ref[seq * n_pages + j]
            out.append(pltpu.make_async_copy(ck_hbm.at[layer, pid], kbuf.at[slot, j], ksem.at[slot]))
            out.append(pltpu.make_async_copy(cv_hbm.at[layer, pid], vbuf.at[slot, j], vsem.at[slot]))
        return out

    slot = b % 2

    @pl.when(b == 0)
    def _():
        for cp in copies(0, 0):
            cp.start()

    @pl.when(b + 1 < nb)
    def _():
        for cp in copies(b + 1, 1 - slot):
            cp.start()

    lfn = -_softplus(-(f_ref[...] + b_ref[...]))
    lfn_ref[...] = lfn
    sub8 = lax.broadcasted_iota(jnp.int32, (n_heads, LANES), 0)
    lane8 = lax.broadcasted_iota(jnp.int32, (n_heads, LANES), 1)
    lfn_col = jnp.sum(jnp.where(sub8 == lane8, jnp.broadcast_to(lfn, (n_heads, LANES)), 0.0),
                      axis=-1, keepdims=True)

    pages = [lfp_ref[:, p * page:(p + 1) * page] for p in range(n_pages)]
    within = _dot3(jnp.concatenate(pages, axis=0), tri_ref[...])
    blocks = [None] * n_pages
    carry = lfn_col
    for p in reversed(range(n_pages)):
        blocks[p] = within[p * n_heads:(p + 1) * n_heads] + carry
        carry = carry + jnp.sum(pages[p], axis=-1, keepdims=True)
    bias = jnp.concatenate(blocks, axis=1)

    q = q_ref[...]
    s_new = jnp.sum(q * kn_ref[...], axis=-1, keepdims=True)
    q16 = jnp.concatenate([q, jnp.zeros_like(q)], axis=0).astype(BF16)
    p_len = n_pages * page
    head_row = lax.broadcasted_iota(jnp.int32, (n_heads, p_len), 0)

    for cp in copies(b, slot):
        cp.wait()

    def head_mat(buf, h):
        return jnp.concatenate([buf[slot, j, h] for j in range(n_pages)], axis=1).astype(BF16)

    s = bias
    for h in range(n_heads):
        s_h = _dot(q16, head_mat(kbuf, h))[0:n_heads]
        s = s + jnp.where(head_row == h, s_h, 0.0)
    m = jnp.maximum(jnp.max(s, axis=-1, keepdims=True), s_new)
    p_past = jnp.exp(s - m)
    p_new = jnp.exp(s_new - m)
    z = jnp.sum(p_past, axis=-1, keepdims=True) + p_new
    p16 = jnp.concatenate([p_past, jnp.zeros_like(p_past)], axis=0).astype(BF16)
    out_row = lax.broadcasted_iota(jnp.int32, (n_heads, HEAD_DIM), 0)
    o = p_new * vn_ref[...]
    for h in range(n_heads):
        o_h = _dot_nt(p16, head_mat(vbuf, h))[0:n_heads]
        o = o + jnp.where(out_row == h, o_h, 0.0)
    o_ref[...] = o / z


def _decode_attention(page_table, q, k_new, v_new, f_raw, b_pad, lfp_t, cache_kt, cache_vt, *, layer):
    bs, n_heads, _ = q.shape
    n_pages = page_table.shape[1]
    page = cache_kt.shape[-1]
    p_len = n_pages * page
    tri = (jnp.arange(page)[:, None] > jnp.arange(page)[None, :]).astype(BF16)
    hrow = pl.BlockSpec((None, n_heads, HEAD_DIM), lambda i, pt: (i, 0, 0))
    frow = pl.BlockSpec((None, 1, LANES), lambda i, pt: (i, 0, 0))
    const = lambda a: pl.BlockSpec(a.shape, lambda i, pt: (0,) * a.ndim)
    kern = functools.partial(_decode_kernel, layer=layer, n_pages=n_pages, page=page, n_heads=n_heads)
    buf = pltpu.VMEM((2, n_pages, n_heads, HEAD_DIM, page), F32)
    grid_spec = pltpu.PrefetchScalarGridSpec(
        num_scalar_prefetch=1, grid=(bs,),
        in_specs=[hrow, hrow, hrow, frow, const(b_pad),
                  pl.BlockSpec((None, n_heads, p_len), lambda i, pt: (i, 0, 0)), const(tri),
                  pl.BlockSpec(memory_space=pl.ANY), pl.BlockSpec(memory_space=pl.ANY)],
        out_specs=[hrow, frow],
        scratch_shapes=[buf, buf, pltpu.SemaphoreType.DMA((2,)), pltpu.SemaphoreType.DMA((2,))])
    o, lfn = pl.pallas_call(
        kern, grid_spec=grid_spec, name="decode_attention",
        out_shape=[jax.ShapeDtypeStruct((bs, n_heads, HEAD_DIM), F32), jax.ShapeDtypeStruct((bs, 1, LANES), F32)],
        compiler_params=_params("arbitrary"),
    )(page_table.reshape(-1), q, k_new, v_new, f_raw.reshape(bs, 1, LANES), b_pad, lfp_t, tri,
      cache_kt, cache_vt)
    return o, lfn.reshape(bs, LANES)


def _outproj_kernel(x_ref, o_ref, y_ref, ga_ref, gb_ref, woa_ref, wob_ref, gf_ref, wr_hi_ref, wr_lo_ref,
                    br_ref, x2_ref, xn2_ref, route_ref, *, att_padded, d_att, c):
    tm = x_ref.shape[0]
    if att_padded:
        o = o_ref[...]
        lane3 = lax.broadcasted_iota(jnp.int32, o.shape, 2)
        o = jnp.where(lane3 < HEAD_DIM, o, 0.0)
        ssq = jnp.sum(jnp.sum(o * o, axis=0), axis=-1, keepdims=True)
        inv = lax.rsqrt(ssq / d_att + EPS)
        y = jnp.zeros((tm, x_ref.shape[1]), F32)
        for h in range(o.shape[0]):
            y = y + _dot((o[h] * inv * ga_ref[h]).astype(BF16), woa_ref[h])
    else:
        y = _dot(_rms(o_ref[...], ga_ref[...]).astype(BF16), woa_ref[...])
    y = y + _dot(_rms(y_ref[:, 0:c], gb_ref[:, 0:c]).astype(BF16), wob_ref[0:c, :])
    y = y + _dot(_rms(y_ref[:, c:2 * c], gb_ref[:, c:2 * c]).astype(BF16), wob_ref[c:2 * c, :])
    x2 = x_ref[...] + y
    x2_ref[...] = x2
    xn2 = _rms(x2, gf_ref[...])
    n_chunks = xn2.shape[1] // LANES
    for ch in range(n_chunks):
        xn2_ref[pl.ds(ch, tm, stride=n_chunks), :] = xn2[:, ch * LANES:(ch + 1) * LANES]

    hi = xn2.astype(BF16)
    lo = (xn2 - hi.astype(F32)).astype(BF16)
    logit = _dot(hi, wr_hi_ref[...]) + _dot(lo, wr_hi_ref[...]) + _dot(hi, wr_lo_ref[...]) + br_ref[...]
    lane = lax.broadcasted_iota(jnp.int32, (tm, LANES), 1).astype(F32)
    far = float(LANES)

    def first_argmax(vals, valid):
        mx = jnp.max(jnp.where(valid, vals, NEG), axis=-1, keepdims=True)
        idx = jnp.min(jnp.where(valid & (vals == mx), lane, far), axis=-1, keepdims=True)
        return mx, idx

    gmask = lane < N_GROUPS
    gmax, gidx = first_argmax(logit, gmask)
    pg_top = 1.0 / jnp.sum(jnp.where(gmask, jnp.exp(logit - gmax), 0.0), axis=-1, keepdims=True)
    e_lo = N_GROUPS + EXPERTS_PER_GROUP * gidx
    emask = (lane >= e_lo) & (lane < e_lo + EXPERTS_PER_GROUP)
    m1, i1 = first_argmax(logit, emask)
    m2, i2 = first_argmax(logit, emask & (lane != i1))
    e = jnp.exp(m2 - m1)
    g1 = pg_top / (1.0 + e)
    g2 = pg_top * e / (1.0 + e)
    route_ref[...] = jnp.where(lane == 0, i1 - N_GROUPS, jnp.where(lane == 1, i2 - N_GROUPS,
                               jnp.where(lane == 2, g1, jnp.where(lane == 3, g2, 0.0))))


def _outproj(x, o, y, ow, *, tm, att_padded):
    b, t, d = x.shape
    c = y.shape[-1] // 2
    d_att = 512
    row = lambda n: pl.BlockSpec((None, tm, n), lambda i, j: (i, j, 0))
    const = lambda a: pl.BlockSpec(a.shape, lambda i, j: (0,) * a.ndim)
    if att_padded:
        o_spec = pl.BlockSpec((None, o.shape[1], tm, LANES), lambda i, j: (i, 0, j, 0))
    else:
        o_spec = row(d_att)
    kern = functools.partial(_outproj_kernel, att_padded=att_padded, d_att=d_att, c=c)
    return pl.pallas_call(
        kern, grid=(b, t // tm), name="outproj_heads" if att_padded else "outproj_flat",
        in_specs=[row(d), o_spec, row(2 * c)] + [const(a) for a in ow],
        out_specs=[row(d), pl.BlockSpec((None, tm * d // LANES, LANES), lambda i, j: (i, j, 0)), row(LANES)],
        out_shape=[jax.ShapeDtypeStruct((b, t, d), F32), jax.ShapeDtypeStruct((b, t * d // LANES, LANES), F32),
                   jax.ShapeDtypeStruct((b, t, LANES), F32)],
        compiler_params=_params("parallel", "parallel"),
    )(x, o, y, *ow)


def _rank_kernel(route_ref, cin_ref, tri_ref, rank_ref, cnt_ref, carry_ref):
    @pl.when(pl.program_id(0) == 0)
    def _():
        carry_ref[...] = cin_ref[...]

    route = route_ref[...]
    tm = route.shape[0]
    lane = lax.broadcasted_iota(jnp.int32, (tm, LANES), 1).astype(F32)
    is1 = lane == route[:, 0:1]
    is2 = lane == route[:, 1:2]
    onehot = jnp.where(is1, 1.0, jnp.where(is2, 1.0, 0.0))
    before = _dot(tri_ref[...], onehot.astype(BF16)) + carry_ref[...]
    r1 = jnp.sum(jnp.where(is1, before, 0.0), axis=-1, keepdims=True)
    r2 = jnp.sum(jnp.where(is2, before, 0.0), axis=-1, keepdims=True)
    rank_ref[...] = jnp.where(lane == 0, r1, jnp.where(lane == 1, r2, 0.0))
    carry_ref[...] = carry_ref[...] + jnp.sum(onehot, axis=0, keepdims=True)
    cnt_ref[...] = carry_ref[...]


def _rank(route, counts_in, *, tm):
    n = route.shape[0]
    tri = (jnp.arange(tm)[:, None] > jnp.arange(tm)[None, :]).astype(BF16)
    return pl.pallas_call(
        _rank_kernel, grid=(n // tm,), name="rank",
        in_specs=[pl.BlockSpec((tm, LANES), lambda i: (i, 0)), pl.BlockSpec((1, LANES), lambda i: (0, 0)),
                  pl.BlockSpec((tm, tm), lambda i: (0, 0))],
        out_specs=[pl.BlockSpec((tm, LANES), lambda i: (i, 0)), pl.BlockSpec((1, LANES), lambda i: (0, 0))],
        out_shape=[jax.ShapeDtypeStruct((n, LANES), F32), jax.ShapeDtypeStruct((1, LANES), F32)],
        scratch_shapes=[pltpu.VMEM((1, LANES), F32)],
        compiler_params=_params("arbitrary"),
    )(route, counts_in, tri)


def _dispatch(e_all, rank_all, counts, n_blocks):
    blk = MOE_BLOCK
    nk = e_all.shape[0] * TOP_K
    nblk_e = (counts + blk - 1) // blk
    bend = jnp.cumsum(nblk_e)
    bstart = bend - nblk_e
    onehot = e_all[:, :, None] == jnp.arange(N_EXPERTS, dtype=jnp.int32)[None, None, :]
    dest = rank_all + jnp.sum(jnp.where(onehot, (bstart * blk)[None, None, :], 0), axis=-1)
    src = jnp.zeros((n_blocks * blk,), jnp.int32).at[dest.reshape(nk)].set(jnp.arange(nk, dtype=jnp.int32))
    n_used = bend[-1].astype(jnp.int32)
    bidx = jnp.arange(n_blocks, dtype=jnp.int32)
    blk_e = jnp.minimum(jnp.sum((bend[None, :] <= bidx[:, None]).astype(jnp.int32), axis=1), N_EXPERTS - 1)
    eq = blk_e[:, None] == jnp.arange(N_EXPERTS, dtype=jnp.int32)[None, :]
    cnt_b = jnp.sum(jnp.where(eq, counts[None, :], 0), axis=1)
    start_b = jnp.sum(jnp.where(eq, bstart[None, :], 0), axis=1)
    n_valid = jnp.clip(cnt_b - (bidx - start_b) * blk, 0, blk)
    n_valid = jnp.where(bidx < n_used, n_valid, 0).astype(jnp.int32)
    last_e = jnp.sum(jnp.where(bidx == n_used - 1, blk_e, 0))
    blk_e = jnp.where(bidx < n_used, blk_e, last_e).astype(jnp.int32)
    return blk_e, n_valid, n_used.reshape(1), src


def _for_rows(n, fn):
    n_main = n // DMA_UNROLL

    def main(t, _):
        for u in range(DMA_UNROLL):
            fn(t * DMA_UNROLL + u)
        return 0

    def rest(r, _):
        fn(r)
        return 0

    lax.fori_loop(0, n_main, main, 0)
    lax.fori_loop(n_main * DMA_UNROLL, n, rest, 0)


def _moe_kernel(be_ref, nv_ref, nu_ref, src_ref, xn_hbm, wg_ref, wu_ref, wd_ref, y_hbm,
                xbuf, ybuf, gsem, ssem, wgb, wub, wdb, *, blk, n_chunks):
    i = pl.program_id(0)
    n_used = nu_ref[0]
    slot = i % 2

    def rows(t):
        return pl.ds(pl.multiple_of(t * n_chunks, n_chunks), n_chunks)

    def gather_copy(bi, sl, r):
        tok = lax.shift_right_logical(src_ref[bi * blk + r], TOP_K.bit_length() - 1)
        return pltpu.make_async_copy(xn_hbm.at[rows(tok)], xbuf.at[sl, rows(r)], gsem.at[sl])

    def scatter_copy(bi, sl, r):
        return pltpu.make_async_copy(ybuf.at[sl, rows(r)], y_hbm.at[rows(src_ref[bi * blk + r])], ssem.at[sl])

    def start_all(make, bi, sl):
        _for_rows(nv_ref[bi], lambda r: make(bi, sl, r).start())

    def wait_all(make, bi, sl):
        _for_rows(nv_ref[bi], lambda r: make(bi, sl, 0).wait())

    @pl.when(i == 0)
    def _():
        xbuf[...] = jnp.zeros_like(xbuf)

        @pl.when(n_used > 0)
        def _():
            start_all(gather_copy, 0, 0)

    @pl.when(i + 1 < n_used)
    def _():
        start_all(gather_copy, i + 1, 1 - slot)

    @pl.when(i < n_used)
    def _():
        @pl.when((i == 0) | (be_ref[i] != be_ref[jnp.maximum(i - 1, 0)]))
        def _():
            wgb[...] = wg_ref[...].astype(BF16)
            wub[...] = wu_ref[...].astype(BF16)
            wdb[...] = wd_ref[...].astype(BF16)

        wait_all(gather_copy, i, slot)
        x = jnp.concatenate([xbuf[slot, pl.ds(ch, blk, stride=n_chunks), :] for ch in range(n_chunks)],
                            axis=1).astype(BF16)
        g = _dot(x, wgb[...])
        u = _dot(x, wub[...])
        hmid = (g * _sigmoid(g) * u).astype(BF16)
        y = _dot(hmid, wdb[...])

        @pl.when(i >= 2)
        def _():
            wait_all(scatter_copy, i - 2, slot)

        for ch in range(n_chunks):
            ybuf[slot, pl.ds(ch, blk, stride=n_chunks), :] = y[:, ch * LANES:(ch + 1) * LANES]
        start_all(scatter_copy, i, slot)

        @pl.when(i == n_used - 1)
        def _():
            @pl.when(i >= 1)
            def _():
                wait_all(scatter_copy, i - 1, 1 - slot)

            wait_all(scatter_copy, i, slot)


def _moe_experts(blk_e, n_valid, n_used, src, xn_all, w_gate, w_up, w_down, *, layer, n_blocks):
    d = w_gate.shape[-2]
    n_chunks = d // LANES
    n_all = xn_all.shape[0] // n_chunks
    de = w_gate.shape[-1]
    blk = MOE_BLOCK
    wspec = lambda r, c_: pl.BlockSpec((None, None, r, c_), lambda i, be, nv, nu, s: (layer, be[i], 0, 0))
    grid_spec = pltpu.PrefetchScalarGridSpec(
        num_scalar_prefetch=4, grid=(n_blocks,),
        in_specs=[pl.BlockSpec(memory_space=pl.ANY), wspec(d, de), wspec(d, de), wspec(de, d)],
        out_specs=pl.BlockSpec(memory_space=pl.ANY),
        scratch_shapes=[pltpu.VMEM((2, blk * n_chunks, LANES), F32), pltpu.VMEM((2, blk * n_chunks, LANES), F32),
                        pltpu.SemaphoreType.DMA((2,)), pltpu.SemaphoreType.DMA((2,)),
                        pltpu.VMEM((d, de), BF16), pltpu.VMEM((d, de), BF16), pltpu.VMEM((de, d), BF16)])
    return pl.pallas_call(
        functools.partial(_moe_kernel, blk=blk, n_chunks=n_chunks), grid_spec=grid_spec, name="moe_experts",
        out_shape=jax.ShapeDtypeStruct((n_all * TOP_K * n_chunks, LANES), F32),
        compiler_params=_params("arbitrary"),
    )(blk_e, n_valid, n_used, src, xn_all, w_gate, w_up, w_down)


def _final_kernel(x2_ref, y_ref, route_ref, g_ref, o_ref):
    o_ref[...] = _rms(_moe_combine(x2_ref[...], route_ref[...], y_ref), g_ref[...])


def _final(x2, y, route, g, *, tm, y_block_offset):
    n, d = x2.shape
    return pl.pallas_call(
        _final_kernel, grid=(n // tm,), name="final",
        in_specs=[pl.BlockSpec((tm, d), lambda i: (i, 0)),
                  pl.BlockSpec((tm * TOP_K * d // LANES, LANES), lambda i: (y_block_offset + i, 0)),
                  pl.BlockSpec((tm, LANES), lambda i: (i, 0)), pl.BlockSpec((1, d), lambda i: (0, 0))],
        out_specs=pl.BlockSpec((tm, d), lambda i: (i, 0)),
        out_shape=jax.ShapeDtypeStruct((n, d), F32),
        compiler_params=_params("parallel"),
    )(x2, y, route, g)


def _block_diag(w):
    n, a, b = w.shape
    out = jnp.zeros((n * a, n * b), w.dtype)
    for i in range(n):
        out = out.at[i * a:(i + 1) * a, i * b:(i + 1) * b].set(w[i])
    return out


def _pad_lanes(v, n=LANES):
    return jnp.zeros((1, n), F32).at[0, :v.shape[0]].set(v)


def _tile_for(t, pref):
    tm = min(pref, t)
    while t % tm:
        tm //= 2
    return tm


def kernel(x_prompt, x_sample, cache_k, cache_v, cache_logf, state_conv, state_rglru, state_pool, page_table, norm_mix, w_in, b_forget, conv_w, conv_b, w_gate_a, b_gate_a, w_gate_x, b_gate_x, lru_lambda, w_pool, pool_scale, norm_groups, w_out, norm_ffn, w_router_group, b_router_group, w_router_expert, b_router_expert, w_up, w_gate, w_down, norm_final):
    depth = w_in.shape[0]
    b_p, t_p, d = x_prompt.shape
    b_s = x_sample.shape[0]
    n_heads = b_forget.shape[1]
    d_att = n_heads * HEAD_DIM
    c = conv_w.shape[-1]
    n_p = b_p * t_p
    n_all = n_p + b_s
    n_pages, page = page_table.shape[1], cache_k.shape[2]
    past_len = n_pages * page
    nk = n_all * TOP_K
    n_blocks = -(-(nk + N_EXPERTS * (MOE_BLOCK - 1)) // MOE_BLOCK)

    tm_p = _tile_for(t_p, 512)
    tk = _tile_for(t_p, 512)
    n_chains = 2 if t_p % (2 * tk) == 0 else 1
    tr = _tile_for(t_p, 256)
    tm_r = _tile_for(n_p, 1024)
    tm_f = _tile_for(n_p, 512)

    cache_kt = cache_k.transpose(0, 1, 3, 4, 2)
    cache_vt = cache_v.transpose(0, 1, 3, 4, 2)
    x_p = x_prompt
    x_s = x_sample.reshape(1, b_s, d)
    moe_p = moe_s = None
    off_f = 3 * d_att
    outs = [[] for _ in range(12)]
    for l in range(depth):
        wl = w_in[l]
        w_cat = jnp.concatenate(
            [wl[:, :off_f], wl[:, off_f + n_heads:], wl[:, off_f:off_f + n_heads],
             jnp.zeros((d, LANES - n_heads), F32)], axis=1).astype(BF16)
        g_mix = norm_mix[l].reshape(1, d)
        b_pad = _pad_lanes(b_forget[l])
        mw = (conv_w[l], conv_b[l].reshape(1, c), _block_diag(w_gate_a[l]).astype(BF16),
              b_gate_a[l].reshape(1, c), _block_diag(w_gate_x[l]).astype(BF16), b_gate_x[l].reshape(1, c),
              lru_lambda[l].reshape(1, c), _block_diag(w_pool[l]).astype(BF16), pool_scale[l].reshape(1, c))
        ng, wo = norm_groups[l], w_out[l]
        ga_pad = jnp.zeros((n_heads, 1, LANES), F32).at[:, 0, :HEAD_DIM].set(ng[:d_att].reshape(n_heads, HEAD_DIM))
        woa_pad = jnp.zeros((n_heads, LANES, d), F32).at[:, :HEAD_DIM].set(
            wo[:d_att].reshape(n_heads, HEAD_DIM, d)).astype(BF16)
        w_r = jnp.zeros((d, LANES), F32).at[:, :N_GROUPS].set(w_router_group[l]).at[
            :, N_GROUPS:N_GROUPS + N_EXPERTS].set(w_router_expert[l])
        wr_hi = w_r.astype(BF16)
        wr_lo = (w_r - wr_hi.astype(F32)).astype(BF16)
        b_r = _pad_lanes(jnp.concatenate([b_router_group[l], b_router_expert[l]]))
        ow_rest = (wo[d_att:].astype(BF16), norm_ffn[l].reshape(1, d), wr_hi, wr_lo, b_r)
        g_rest = ng[d_att:].reshape(1, 2 * c)
        ow_p = (ga_pad, g_rest, woa_pad, *ow_rest)
        ow_s = (ng[:d_att].reshape(1, d_att), g_rest, wo[:d_att].astype(BF16), *ow_rest)

        res = _inproj(x_p, g_mix, w_cat, tm=tm_p, head_layout=True, moe=moe_p, b_pad=b_pad)
        if moe_p is not None:
            x_p, res = res[0], res[1:]
        qa, ka, vp, k_p, v_p, lf_p, rest_p = res
        o_p = _attention(qa, ka, vp, tk=tk, n_chains=n_chains)
        y_mix_p, conv_t, h_t, pool_t = _mix_prompt(rest_p, mw, tr=tr)
        x2_p, xn2_p, route_p = _outproj(x_p, o_p, y_mix_p, ow_p, tm=tm_p, att_padded=True)

        res = _inproj(x_s, g_mix, w_cat, tm=b_s, head_layout=False, moe=moe_s, y_block_offset=n_p // b_s)
        if moe_s is not None:
            x_s, res = res[0], res[1:]
        q_s, k_s, v_s, f_s, rest_s = res
        heads = lambda a: a.reshape(b_s, n_heads, HEAD_DIM)
        lfp_t = cache_logf[l][page_table].reshape(b_s, past_len, n_heads).transpose(0, 2, 1)
        o_s, lf_s = _decode_attention(page_table, heads(q_s), heads(k_s), heads(v_s), f_s[0], b_pad, lfp_t,
                                      cache_kt, cache_vt, layer=l)
        y_mix_s, conv_s, h_s, pool_s = _mix_sample(
            rest_s[0], state_conv[l].transpose(1, 0, 2), state_rglru[l], state_pool[l].transpose(1, 0, 2),
            mw, pos0=past_len)
        x2_s, xn2_s, route_s = _outproj(x_s, o_s.reshape(1, b_s, d_att), y_mix_s[None], ow_s, tm=b_s,
                                        att_padded=False)

        route_p2, route_s2 = route_p.reshape(n_p, LANES), route_s.reshape(b_s, LANES)
        rank_p, cnt_p = _rank(route_p2, jnp.zeros((1, LANES), F32), tm=tm_r)
        rank_s, cnt = _rank(route_s2, cnt_p, tm=b_s)
        first2 = lambda a_p, a_s: jnp.concatenate([a_p[:, :TOP_K], a_s[:, :TOP_K]], axis=0).astype(jnp.int32)
        blk_e, n_valid, n_used, src = _dispatch(first2(route_p2, route_s2), first2(rank_p, rank_s),
                                                cnt[0, :N_EXPERTS].astype(jnp.int32), n_blocks)
        xn2_all = jnp.concatenate([xn2_p.reshape(-1, LANES), xn2_s.reshape(-1, LANES)], axis=0)
        y_moe = _moe_experts(blk_e, n_valid, n_used, src, xn2_all, w_gate, w_up, w_down,
                             layer=l, n_blocks=n_blocks)
        x_p, x_s = x2_p, x2_s
        moe_p, moe_s = (y_moe, route_p), (y_moe, route_s)

        per_layer = (
            k_p.reshape(b_p, t_p, n_heads, HEAD_DIM), v_p.reshape(b_p, t_p, n_heads, HEAD_DIM),
            lf_p[:, :, :n_heads], conv_t[:, 8 - (CONV_W - 1):], h_t[:, 7], pool_t[:, 16 - POOL_BUF:],
            k_s.reshape(b_s, 1, n_heads, HEAD_DIM), v_s.reshape(b_s, 1, n_heads, HEAD_DIM),
            lf_s[:, None, :n_heads], conv_s.transpose(1, 0, 2), h_s, pool_s.transpose(1, 0, 2))
        for acc, val in zip(outs, per_layer):
            acc.append(val)
    g_fin = norm_final.reshape(1, d)
    y_prompt = _final(x_p.reshape(n_p, d), moe_p[0], moe_p[1].reshape(n_p, LANES), g_fin,
                      tm=tm_f, y_block_offset=0).reshape(b_p, t_p, d)
    y_sample = _final(x_s.reshape(b_s, d), moe_s[0], moe_s[1].reshape(b_s, LANES), g_fin,
                      tm=b_s, y_block_offset=n_p // b_s).reshape(b_s, 1, d)
    return (y_prompt, y_sample, *[jnp.stack(o) for o in outs])
```

```python
import functools

import jax
import jax.numpy as jnp
from jax import lax
from jax.experimental import pallas as pl
from jax.experimental.pallas import tpu as pltpu

F32 = jnp.float32
BF16 = jnp.bfloat16

EPS = 1e-6
HEAD_DIM = 64
LANES = 128
RG_C = 8.0
CONV_W = 4
POOL_WINDOWS = (2, 4, 8, 16)
POOL_BUF = max(POOL_WINDOWS) - 1
N_GROUPS = 4
EXPERTS_PER_GROUP = 8
N_EXPERTS = N_GROUPS * EXPERTS_PER_GROUP
TOP_K = 2
MOE_BLOCK = 256
DMA_UNROLL = 8
NEG = -1e30
LOG2E = 1.4426950408889634
N_BIAS = 3
VMEM_LIMIT = 56 * 1024 * 1024


def _params(*sem):
    return pltpu.CompilerParams(dimension_semantics=sem, vmem_limit_bytes=VMEM_LIMIT)


def _split3(x):
    hi = x.astype(BF16)
    r1 = x - hi.astype(F32)
    mid = r1.astype(BF16)
    lo = (r1 - mid.astype(F32)).astype(BF16)
    return hi, mid, lo


def _dot(a, b):
    return jnp.dot(a, b, preferred_element_type=F32)


def _dot_nt(a, b):
    return lax.dot_general(a, b, (((1,), (1,)), ((), ())), preferred_element_type=F32)


def _dot3(x, w_bf16):
    hi, mid, lo = _split3(x)
    return _dot(hi, w_bf16) + _dot(mid, w_bf16) + _dot(lo, w_bf16)


def _dot3_left(w_bf16, x):
    hi, mid, lo = _split3(x)
    return _dot(w_bf16, hi) + _dot(w_bf16, mid) + _dot(w_bf16, lo)


def _sigmoid(x):
    return 1.0 / (1.0 + jnp.exp(-x))


def _softplus(x):
    return jnp.maximum(x, 0.0) + jnp.log1p(jnp.exp(-jnp.abs(x)))


def _gelu_tanh(x):
    return 0.5 * x * (1.0 + jnp.tanh(0.7978845608028654 * (x + 0.044715 * (x * x * x))))


def _rms(x, g):
    return x * lax.rsqrt(jnp.mean(x * x, axis=-1, keepdims=True) + EPS) * g


def _moe_combine(x2, route, y_ref):
    tm, d = x2.shape
    n_chunks = d // LANES
    g1, g2 = route[:, 2:3], route[:, 3:4]
    step = TOP_K * n_chunks
    out = []
    for c in range(n_chunks):
        y0 = y_ref[pl.ds(c, tm, stride=step), :]
        y1 = y_ref[pl.ds(n_chunks + c, tm, stride=step), :]
        out.append(x2[:, c * LANES:(c + 1) * LANES] + (g1 * y0 + g2 * y1))
    return jnp.concatenate(out, axis=1)


def _inproj_kernel(*refs, d_att, d_rest, head_layout, has_moe, scale):
    refs = list(refs)
    x_ref = refs.pop(0)
    if has_moe:
        y_ref, route_ref = refs.pop(0), refs.pop(0)
    g_ref, w_ref = refs.pop(0), refs.pop(0)
    if head_layout:
        b_ref, tri_ref, place_ref = refs.pop(0), refs.pop(0), refs.pop(0)
    x = x_ref[...]
    if has_moe:
        x = _moe_combine(x, route_ref[...], y_ref)
        refs.pop(0)[...] = x
    xn = _rms(x, g_ref[...]).astype(BF16)

    def sec(lo, n):
        return _dot(xn, w_ref[:, lo:lo + n])

    q = sec(0, d_att) * scale
    k = sec(d_att, d_att)
    v = sec(2 * d_att, d_att)
    rest = sec(3 * d_att, d_rest)
    f = sec(3 * d_att + d_rest, LANES)
    if not head_layout:
        q_ref, k_ref, v_ref, f_ref, r_ref = refs
        q_ref[...] = q
        k_ref[...] = k
        v_ref[...] = v
        f_ref[...] = f
        r_ref[...] = rest
        return
    qa_ref, ka_ref, vp_ref, k_ref, v_ref, lf_ref, r_ref, carry_ref = refs
    k_ref[...] = k
    v_ref[...] = v
    r_ref[...] = rest

    @pl.when(pl.program_id(1) == 0)
    def _():
        carry_ref[...] = jnp.zeros_like(carry_ref)

    lf = -_softplus(-(f + b_ref[...]))
    lf_ref[...] = lf
    cs = _dot3_left(tri_ref[...], lf) + carry_ref[...]
    carry_ref[...] = carry_ref[...] + jnp.sum(lf, axis=0, keepdims=True)

    tm = q.shape[0]
    n_heads = d_att // HEAD_DIM
    lane = lax.broadcasted_iota(jnp.int32, (tm, LANES), 1)
    parts = [p.astype(F32) for p in _split3(cs * (-LOG2E))]
    packed = jnp.where(lane < n_heads, parts[0], 0.0)
    for j in range(1, N_BIAS):
        in_j = (lane >= j * n_heads) & (lane < (j + 1) * n_heads)
        packed = packed + jnp.where(in_j, pltpu.roll(parts[j], j * n_heads, axis=1), 0.0)
    aug = _dot(packed.astype(BF16), place_ref[...])

    low = lane < HEAD_DIM
    q_fill = jnp.where((lane >= HEAD_DIM) & (lane < HEAD_DIM + N_BIAS), 1.0, 0.0)
    v_fill = jnp.where(lane == HEAD_DIM, 1.0, 0.0)
    for j in range(d_att // LANES):
        sl = slice(j * LANES, (j + 1) * LANES)
        for val, ref, fills in ((q, qa_ref, (q_fill, q_fill)), (v, vp_ref, (v_fill, v_fill)),
                                (k, ka_ref, (aug[:, 2 * j * LANES:(2 * j + 1) * LANES],
                                             aug[:, (2 * j + 1) * LANES:(2 * j + 2) * LANES]))):
            pair = val[:, sl]
            ref[2 * j] = jnp.where(low, pair, fills[0]).astype(BF16)
            ref[2 * j + 1] = jnp.where(low, pltpu.roll(pair, HEAD_DIM, axis=1), fills[1]).astype(BF16)


def _bias_placement(n_heads):
    rows = jnp.arange(LANES)[:, None]
    cols = jnp.arange(n_heads * LANES)[None, :]
    j, h = rows // n_heads, rows % n_heads
    hit = (rows < N_BIAS * n_heads) & (cols == h * LANES + HEAD_DIM + j)
    return hit.astype(BF16)


def _inproj(x, g, w, *, tm, head_layout, moe=None, b_pad=None, y_block_offset=0):
    b, t, d = x.shape
    n_w = w.shape[1]
    d_att = 512
    d_rest = n_w - 3 * d_att - LANES
    n_heads = d_att // HEAD_DIM
    nt = t // tm
    row = lambda n: pl.BlockSpec((None, tm, n), lambda i, j: (i, j, 0))
    const = lambda a: pl.BlockSpec(a.shape, lambda i, j: (0,) * a.ndim)
    slab = pl.BlockSpec((None, n_heads, tm, LANES), lambda i, j: (i, 0, j, 0))
    f32s = lambda n: jax.ShapeDtypeStruct((b, t, n), F32)
    args, in_specs = [x], [row(d)]
    out_shape, out_specs = [], []
    if moe is not None:
        y, route = moe
        args += [y, route]
        in_specs += [pl.BlockSpec((tm * TOP_K * d // LANES, LANES), lambda i, j: (y_block_offset + i * nt + j, 0)),
                     row(LANES)]
        out_shape.append(f32s(d))
        out_specs.append(row(d))
    args += [g, w]
    in_specs += [const(g), const(w)]
    scratch = []
    if head_layout:
        tri = (jnp.arange(tm)[:, None] >= jnp.arange(tm)[None, :]).astype(BF16)
        place = _bias_placement(n_heads)
        args += [b_pad, tri, place]
        in_specs += [const(b_pad), const(tri), const(place)]
        slab_s = jax.ShapeDtypeStruct((b, n_heads, t, LANES), BF16)
        out_shape += [slab_s, slab_s, slab_s]
        out_specs += [slab, slab, slab]
        scratch = [pltpu.VMEM((1, LANES), F32)]
        scale = HEAD_DIM ** -0.5 * LOG2E
    else:
        out_shape.append(f32s(d_att))
        out_specs.append(row(d_att))
        scale = HEAD_DIM ** -0.5
    out_shape += [f32s(d_att), f32s(d_att), f32s(LANES), f32s(d_rest)]
    out_specs += [row(d_att), row(d_att), row(LANES), row(d_rest)]
    kern = functools.partial(_inproj_kernel, d_att=d_att, d_rest=d_rest, head_layout=head_layout,
                             has_moe=moe is not None, scale=scale)
    return pl.pallas_call(
        kern, grid=(b, nt), name="inproj_heads" if head_layout else "inproj_flat",
        in_specs=in_specs, out_specs=out_specs, out_shape=out_shape, scratch_shapes=scratch,
        compiler_params=_params("parallel", "arbitrary"),
    )(*args)


def _attention_kernel(q_ref, k_ref, v_ref, o_ref, *, tk, n_chains):
    qi = pl.program_id(2)
    qs = [q_ref[c * tk:(c + 1) * tk, :] for c in range(n_chains)]

    def kv(j):
        off = pl.multiple_of(j * tk, tk)
        return k_ref[pl.ds(off, tk), :], v_ref[pl.ds(off, tk), :]

    def tile(q, k, v, m, acc, masked):
        s = _dot_nt(q, k)
        if masked:
            row = lax.broadcasted_iota(jnp.int32, (tk, tk), 0)
            col = lax.broadcasted_iota(jnp.int32, (tk, tk), 1)
            s = jnp.where(row >= col, s, NEG)
        m_new = jnp.maximum(m, jnp.max(s, axis=-1, keepdims=True))
        p = jnp.exp2(s - m_new)
        acc = jnp.exp2(m - m_new) * acc + _dot(p.astype(BF16), v)
        return m_new, acc

    def trip(n_tiles):
        def body(j, carry):
            first, carry = carry
            for d in range(n_tiles):
                k, v = kv(first + j * n_tiles + d)
                carry = tuple(tile(qs[c], k, v, *carry[c], False) for c in range(n_chains))
            return first, carry
        return body

    init = tuple((jnp.full((tk, 1), NEG, F32), jnp.zeros((tk, LANES), F32)) for _ in range(n_chains))
    n_full = qi * n_chains
    long_trip = 2 * n_chains
    n_long = n_full // long_trip
    _, carry = lax.fori_loop(0, n_long, trip(long_trip), (0, init))
    _, carry = lax.fori_loop(0, (n_full - n_long * long_trip) // n_chains, trip(n_chains),
                             (n_long * long_trip, carry))
    carry = list(carry)
    for d in range(n_chains):
        k, v = kv(qi * n_chains + d)
        for c in range(d, n_chains):
            carry[c] = tile(qs[c], k, v, *carry[c], c == d)
    for c in range(n_chains):
        acc = carry[c][1]
        o_ref[c * tk:(c + 1) * tk, :] = acc / acc[:, HEAD_DIM:HEAD_DIM + 1]


def _attention(qa, ka, vp, *, tk, n_chains):
    b, h, t, _ = qa.shape
    tq = tk * n_chains
    tile = pl.BlockSpec((None, None, tq, LANES), lambda i, j, n: (i, j, n, 0))
    full = pl.BlockSpec((None, None, t, LANES), lambda i, j, n: (i, j, 0, 0))
    return pl.pallas_call(
        functools.partial(_attention_kernel, tk=tk, n_chains=n_chains), grid=(b, h, t // tq), name="attention",
        in_specs=[tile, full, full], out_specs=tile,
        out_shape=jax.ShapeDtypeStruct((b, h, t, LANES), F32),
        compiler_params=_params("parallel", "parallel", "arbitrary"),
    )(qa, ka, vp)


def _rglru_gates(xc, gb, wa_ref, ba_ref, wx_ref, bx_ref, lam_ref):
    xcb = xc.astype(BF16)
    r = _sigmoid(_dot(xcb, wa_ref[...]) + ba_ref[...])
    i = _sigmoid(_dot(xcb, wx_ref[...]) + bx_ref[...])
    log_a = -RG_C * r * _softplus(-lam_ref[...])
    a = jnp.exp(log_a)
    u = jnp.sqrt(jnp.tanh(-log_a) * (a * a + 1.0)) * (i * xc)
    return a, u, _gelu_tanh(gb)


def _pool_lane_select(c, w2, w4, w8, w16):
    lane = lax.broadcasted_iota(jnp.int32, c.shape, 1)
    gw = c.shape[1] // len(POOL_WINDOWS)
    return jnp.where(lane < gw, w2, jnp.where(lane < 2 * gw, w4, jnp.where(lane < 3 * gw, w8, w16)))


def _mix_prompt_kernel(r_ref, cw_ref, cb_ref, wa_ref, ba_ref, wx_ref, bx_ref, lam_ref, wp_ref, ps_ref,
                       y_ref, conv_ref, h_ref, pool_ref, xprev_ref, uprev_ref, htail_ref, *, tr, c):
    t = pl.program_id(1)

    @pl.when(t == 0)
    def _():
        xprev_ref[...] = jnp.zeros_like(xprev_ref)
        uprev_ref[...] = jnp.zeros_like(uprev_ref)
        htail_ref[...] = jnp.zeros_like(htail_ref)

    xb = r_ref[:, 0:c]
    gb = r_ref[:, c:2 * c]
    u_in = r_ref[:, 2 * c:3 * c]

    ext = jnp.concatenate([xprev_ref[...], xb], axis=0)
    xc = cb_ref[...] + cw_ref[CONV_W - 1:CONV_W, :] * xb
    for j in range(1, CONV_W):
        xc = xc + cw_ref[CONV_W - 1 - j:CONV_W - j, :] * pltpu.roll(ext, j, axis=0)[8:]

    a, u, gate = _rglru_gates(xc, gb, wa_ref, ba_ref, wx_ref, bx_ref, lam_ref)

    row = lax.broadcasted_iota(jnp.int32, (tr, c), 0)
    big_a, big_u = a, u
    sh = 1
    while sh < tr:
        keep = row >= sh
        a_s = jnp.where(keep, pltpu.roll(big_a, sh, axis=0), 1.0)
        u_s = jnp.where(keep, pltpu.roll(big_u, sh, axis=0), 0.0)
        big_u = big_a * u_s + big_u
        big_a = big_a * a_s
        sh *= 2
    h = big_u + big_a * htail_ref[7:8, :]

    extu = jnp.concatenate([uprev_ref[...], u_in], axis=0)
    s2 = extu + pltpu.roll(extu, 1, axis=0)
    s4 = s2 + pltpu.roll(s2, 2, axis=0)
    s8 = s4 + pltpu.roll(s4, 4, axis=0)
    s16 = s8 + pltpu.roll(s8, 8, axis=0)
    win = _pool_lane_select(u_in, s2[16:], s4[16:], s8[16:], s16[16:])
    wsz = _pool_lane_select(u_in, 2.0, 4.0, 8.0, 16.0)
    pos = (t * tr + row).astype(F32)
    d = win / jnp.minimum(pos + 1.0, wsz) - u_in
    y_pool = _dot(d.astype(BF16), wp_ref[...]) * ps_ref[...]

    y_ref[:, 0:c] = h * gate
    y_ref[:, c:2 * c] = y_pool
    conv_ref[...] = xb[tr - 8:]
    h_ref[...] = h[tr - 8:]
    pool_ref[...] = u_in[tr - 16:]
    xprev_ref[...] = xb[tr - 8:]
    uprev_ref[...] = u_in[tr - 16:]
    htail_ref[...] = h[tr - 8:]


def _mix_prompt(rest, mw, *, tr):
    b, t, c3 = rest.shape
    c = c3 // 3
    vec = lambda n: pl.BlockSpec((n, c), lambda i, j: (0, 0))
    mat = pl.BlockSpec((c, c), lambda i, j: (0, 0))
    tail = lambda n: pl.BlockSpec((None, n, c), lambda i, j: (i, 0, 0))
    return pl.pallas_call(
        functools.partial(_mix_prompt_kernel, tr=tr, c=c), grid=(b, t // tr), name="mix_prompt",
        in_specs=[pl.BlockSpec((None, tr, c3), lambda i, j: (i, j, 0)),
                  vec(CONV_W), vec(1), mat, vec(1), mat, vec(1), vec(1), mat, vec(1)],
        out_specs=[pl.BlockSpec((None, tr, 2 * c), lambda i, j: (i, j, 0)), tail(8), tail(8), tail(16)],
        out_shape=[jax.ShapeDtypeStruct((b, t, 2 * c), F32), jax.ShapeDtypeStruct((b, 8, c), F32),
                   jax.ShapeDtypeStruct((b, 8, c), F32), jax.ShapeDtypeStruct((b, 16, c), F32)],
        scratch_shapes=[pltpu.VMEM((8, c), F32), pltpu.VMEM((16, c), F32), pltpu.VMEM((8, c), F32)],
        compiler_params=_params("parallel", "arbitrary"),
    )(rest, *mw)


def _mix_sample_kernel(r_ref, sc_ref, h0_ref, sp_ref, cw_ref, cb_ref, wa_ref, ba_ref, wx_ref, bx_ref,
                       lam_ref, wp_ref, ps_ref, y_ref, conv_ref, h_ref, pool_ref, *, c, pos0):
    xb = r_ref[:, 0:c]
    gb = r_ref[:, c:2 * c]
    u_in = r_ref[:, 2 * c:3 * c]

    xc = cb_ref[...] + cw_ref[CONV_W - 1:CONV_W, :] * xb
    for j in range(CONV_W - 1):
        xc = xc + cw_ref[j:j + 1, :] * sc_ref[j]
    a, u, gate = _rglru_gates(xc, gb, wa_ref, ba_ref, wx_ref, bx_ref, lam_ref)
    h = a * h0_ref[...] + u

    sums = {}
    acc = u_in
    for n in range(1, POOL_BUF + 1):
        acc = acc + sp_ref[POOL_BUF - n]
        if n + 1 in POOL_WINDOWS:
            sums[n + 1] = acc
    win = _pool_lane_select(u_in, *[sums[w] for w in POOL_WINDOWS])
    cnt = _pool_lane_select(u_in, *[float(min(pos0 + 1, w)) for w in POOL_WINDOWS])
    d = win / cnt - u_in
    y_pool = _dot(d.astype(BF16), wp_ref[...]) * ps_ref[...]

    y_ref[:, 0:c] = h * gate
    y_ref[:, c:2 * c] = y_pool
    h_ref[...] = h
    for j in range(CONV_W - 2):
        conv_ref[j] = sc_ref[j + 1]
    conv_ref[CONV_W - 2] = xb
    for j in range(POOL_BUF - 1):
        pool_ref[j] = sp_ref[j + 1]
    pool_ref[POOL_BUF - 1] = u_in


def _mix_sample(rest, sc_t, h0, sp_t, mw, *, pos0):
    bs, c3 = rest.shape
    c = c3 // 3
    return pl.pallas_call(
        functools.partial(_mix_sample_kernel, c=c, pos0=pos0), name="mix_sample",
        out_shape=[jax.ShapeDtypeStruct((bs, 2 * c), F32), jax.ShapeDtypeStruct(sc_t.shape, F32),
                   jax.ShapeDtypeStruct((bs, c), F32), jax.ShapeDtypeStruct(sp_t.shape, F32)],
        compiler_params=pltpu.CompilerParams(vmem_limit_bytes=VMEM_LIMIT),
    )(rest, sc_t, h0, sp_t, *mw)


def _decode_kernel(pt_ref, q_ref, kn_ref, vn_ref, f_ref, b_ref, lfp_ref, tri_ref, ck_hbm, cv_hbm,
                   o_ref, lfn_ref, kbuf, vbuf, ksem, vsem, *, layer, n_pages, page, n_heads):
    b = pl.program_id(0)
    nb = pl.num_programs(0)

    def copies(seq, slot):
        out = []
        for j in range(n_pages):
            pid = pt_ref[seq * n_pages + j]
            out.append(pltpu.make_async_copy(ck_hbm.at[layer, pid], kbuf.at[slot, j], ksem.at[slot]))
            out.append(pltpu.make_async_copy(cv_hbm.at[layer, pid], vbuf.at[slot, j], vsem.at[slot]))
        return out

    slot = b % 2

    @pl.when(b == 0)
    def _():
        for cp in copies(0, 0):
            cp.start()

    @pl.when(b + 1 < nb)
    def _():
        for cp in copies(b + 1, 1 - slot):
            cp.start()

    lfn = -_softplus(-(f_ref[...] + b_ref[...]))
    lfn_ref[...] = lfn
    sub8 = lax.broadcasted_iota(jnp.int32, (n_heads, LANES), 0)
    lane8 = lax.broadcasted_iota(jnp.int32, (n_heads, LANES), 1)
    lfn_col = jnp.sum(jnp.where(sub8 == lane8, jnp.broadcast_to(lfn, (n_heads, LANES)), 0.0),
                      axis=-1, keepdims=True)

    pages = [lfp_ref[:, p * page:(p + 1) * page] for p in range(n_pages)]
    within = _dot3(jnp.concatenate(pages, axis=0), tri_ref[...])
    blocks = [None] * n_pages
    carry = lfn_col
    for p in reversed(range(n_pages)):
        blocks[p] = within[p * n_heads:(p + 1) * n_heads] + carry
        carry = carry + jnp.sum(pages[p], axis=-1, keepdims=True)
    bias = jnp.concatenate(blocks, axis=1)

    q = q_ref[...]
    s_new = jnp.sum(q * kn_ref[...], axis=-1, keepdims=True)
    q16 = jnp.concatenate([q, jnp.zeros_like(q)], axis=0).astype(BF16)
    p_len = n_pages * page
    head_row = lax.broadcasted_iota(jnp.int32, (n_heads, p_len), 0)

    for cp in copies(b, slot):
        cp.wait()

    def head_mat(buf, h):
        return jnp.concatenate([buf[slot, j, h] for j in range(n_pages)], axis=1).astype(BF16)

    s = bias
    for h in range(n_heads):
        s_h = _dot(q16, head_mat(kbuf, h))[0:n_heads]
        s = s + jnp.where(head_row == h, s_h, 0.0)
    m = jnp.maximum(jnp.max(s, axis=-1, keepdims=True), s_new)
    p_past = jnp.exp(s - m)
    p_new = jnp.exp(s_new - m)
    z = jnp.sum(p_past, axis=-1, keepdims=True) + p_new
    p16 = jnp.concatenate([p_past, jnp.zeros_like(p_past)], axis=0).astype(BF16)
    out_row = lax.broadcasted_iota(jnp.int32, (n_heads, HEAD_DIM), 0)
    o = p_new * vn_ref[...]
    for h in range(n_heads):
        o_h = _dot_nt(p16, head_mat(vbuf, h))[0:n_heads]
        o = o + jnp.where(out_row == h, o_h, 0.0)
    o_ref[...] = o / z


def _decode_attention(page_table, q, k_new, v_new, f_raw, b_pad, lfp_t, cache_kt, cache_vt, *, layer):
    bs, n_heads, _ = q.shape
    n_pages = page_table.shape[1]
    page = cache_kt.shape[-1]
    p_len = n_pages * page
    tri = (jnp.arange(page)[:, None] > jnp.arange(page)[None, :]).astype(BF16)
    hrow = pl.BlockSpec((None, n_heads, HEAD_DIM), lambda i, pt: (i, 0, 0))
    frow = pl.BlockSpec((None, 1, LANES), lambda i, pt: (i, 0, 0))
    const = lambda a: pl.BlockSpec(a.shape, lambda i, pt: (0,) * a.ndim)
    kern = functools.partial(_decode_kernel, layer=layer, n_pages=n_pages, page=page, n_heads=n_heads)
    buf = pltpu.VMEM((2, n_pages, n_heads, HEAD_DIM, page), F32)
    grid_spec = pltpu.PrefetchScalarGridSpec(
        num_scalar_prefetch=1, grid=(bs,),
        in_specs=[hrow, hrow, hrow, frow, const(b_pad),
                  pl.BlockSpec((None, n_heads, p_len), lambda i, pt: (i, 0, 0)), const(tri),
                  pl.BlockSpec(memory_space=pl.ANY), pl.BlockSpec(memory_space=pl.ANY)],
        out_specs=[hrow, frow],
        scratch_shapes=[buf, buf, pltpu.SemaphoreType.DMA((2,)), pltpu.SemaphoreType.DMA((2,))])
    o, lfn = pl.pallas_call(
        kern, grid_spec=grid_spec, name="decode_attention",
        out_shape=[jax.ShapeDtypeStruct((bs, n_heads, HEAD_DIM), F32), jax.ShapeDtypeStruct((bs, 1, LANES), F32)],
        compiler_params=_params("arbitrary"),
    )(page_table.reshape(-1), q, k_new, v_new, f_raw.reshape(bs, 1, LANES), b_pad, lfp_t, tri,
      cache_kt, cache_vt)
    return o, lfn.reshape(bs, LANES)


def _outproj_kernel(x_ref, o_ref, y_ref, ga_ref, gb_ref, woa_ref, wob_ref, gf_ref, wr_hi_ref, wr_lo_ref,
                    br_ref, x2_ref, xn2_ref, route_ref, *, att_padded, d_att, c):
    tm = x_ref.shape[0]
    if att_padded:
        o = o_ref[...]
        lane3 = lax.broadcasted_iota(jnp.int32, o.shape, 2)
        o = jnp.where(lane3 < HEAD_DIM, o, 0.0)
        ssq = jnp.sum(jnp.sum(o * o, axis=0), axis=-1, keepdims=True)
        inv = lax.rsqrt(ssq / d_att + EPS)
        y = jnp.zeros((tm, x_ref.shape[1]), F32)
        for h in range(o.shape[0]):
            y = y + _dot((o[h] * inv * ga_ref[h]).astype(BF16), woa_ref[h])
    else:
        y = _dot(_rms(o_ref[...], ga_ref[...]).astype(BF16), woa_ref[...])
    y = y + _dot(_rms(y_ref[:, 0:c], gb_ref[:, 0:c]).astype(BF16), wob_ref[0:c, :])
    y = y + _dot(_rms(y_ref[:, c:2 * c], gb_ref[:, c:2 * c]).astype(BF16), wob_ref[c:2 * c, :])
    x2 = x_ref[...] + y
    x2_ref[...] = x2
    xn2 = _rms(x2, gf_ref[...])
    n_chunks = xn2.shape[1] // LANES
    for ch in range(n_chunks):
        xn2_ref[pl.ds(ch, tm, stride=n_chunks), :] = xn2[:, ch * LANES:(ch + 1) * LANES]

    hi = xn2.astype(BF16)
    lo = (xn2 - hi.astype(F32)).astype(BF16)
    logit = _dot(hi, wr_hi_ref[...]) + _dot(lo, wr_hi_ref[...]) + _dot(hi, wr_lo_ref[...]) + br_ref[...]
    lane = lax.broadcasted_iota(jnp.int32, (tm, LANES), 1).astype(F32)
    far = float(LANES)

    def first_argmax(vals, valid):
        mx = jnp.max(jnp.where(valid, vals, NEG), axis=-1, keepdims=True)
        idx = jnp.min(jnp.where(valid & (vals == mx), lane, far), axis=-1, keepdims=True)
        return mx, idx

    gmask = lane < N_GROUPS
    gmax, gidx = first_argmax(logit, gmask)
    pg_top = 1.0 / jnp.sum(jnp.where(gmask, jnp.exp(logit - gmax), 0.0), axis=-1, keepdims=True)
    e_lo = N_GROUPS + EXPERTS_PER_GROUP * gidx
    emask = (lane >= e_lo) & (lane < e_lo + EXPERTS_PER_GROUP)
    m1, i1 = first_argmax(logit, emask)
    m2, i2 = first_argmax(logit, emask & (lane != i1))
    e = jnp.exp(m2 - m1)
    g1 = pg_top / (1.0 + e)
    g2 = pg_top * e / (1.0 + e)
    route_ref[...] = jnp.where(lane == 0, i1 - N_GROUPS, jnp.where(lane == 1, i2 - N_GROUPS,
                               jnp.where(lane == 2, g1, jnp.where(lane == 3, g2, 0.0))))


def _outproj(x, o, y, ow, *, tm, att_padded):
    b, t, d = x.shape
    c = y.shape[-1] // 2
    d_att = 512
    row = lambda n: pl.BlockSpec((None, tm, n), lambda i, j: (i, j, 0))
    const = lambda a: pl.BlockSpec(a.shape, lambda i, j: (0,) * a.ndim)
    if att_padded:
        o_spec = pl.BlockSpec((None, o.shape[1], tm, LANES), lambda i, j: (i, 0, j, 0))
    else:
        o_spec = row(d_att)
    kern = functools.partial(_outproj_kernel, att_padded=att_padded, d_att=d_att, c=c)
    return pl.pallas_call(
        kern, grid=(b, t // tm), name="outproj_heads" if att_padded else "outproj_flat",
        in_specs=[row(d), o_spec, row(2 * c)] + [const(a) for a in ow],
        out_specs=[row(d), pl.BlockSpec((None, tm * d // LANES, LANES), lambda i, j: (i, j, 0)), row(LANES)],
        out_shape=[jax.ShapeDtypeStruct((b, t, d), F32), jax.ShapeDtypeStruct((b, t * d // LANES, LANES), F32),
                   jax.ShapeDtypeStruct((b, t, LANES), F32)],
        compiler_params=_params("parallel", "parallel"),
    )(x, o, y, *ow)


def _rank_kernel(route_ref, cin_ref, tri_ref, rank_ref, cnt_ref, carry_ref):
    @pl.when(pl.program_id(0) == 0)
    def _():
        carry_ref[...] = cin_ref[...]

    route = route_ref[...]
    tm = route.shape[0]
    lane = lax.broadcasted_iota(jnp.int32, (tm, LANES), 1).astype(F32)
    is1 = lane == route[:, 0:1]
    is2 = lane == route[:, 1:2]
    onehot = jnp.where(is1, 1.0, jnp.where(is2, 1.0, 0.0))
    before = _dot(tri_ref[...], onehot.astype(BF16)) + carry_ref[...]
    r1 = jnp.sum(jnp.where(is1, before, 0.0), axis=-1, keepdims=True)
    r2 = jnp.sum(jnp.where(is2, before, 0.0), axis=-1, keepdims=True)
    rank_ref[...] = jnp.where(lane == 0, r1, jnp.where(lane == 1, r2, 0.0))
    carry_ref[...] = carry_ref[...] + jnp.sum(onehot, axis=0, keepdims=True)
    cnt_ref[...] = carry_ref[...]


def _rank(route, counts_in, *, tm):
    n = route.shape[0]
    tri = (jnp.arange(tm)[:, None] > jnp.arange(tm)[None, :]).astype(BF16)
    return pl.pallas_call(
        _rank_kernel, grid=(n // tm,), name="rank",
        in_specs=[pl.BlockSpec((tm, LANES), lambda i: (i, 0)), pl.BlockSpec((1, LANES), lambda i: (0, 0)),
                  pl.BlockSpec((tm, tm), lambda i: (0, 0))],
        out_specs=[pl.BlockSpec((tm, LANES), lambda i: (i, 0)), pl.BlockSpec((1, LANES), lambda i: (0, 0))],
        out_shape=[jax.ShapeDtypeStruct((n, LANES), F32), jax.ShapeDtypeStruct((1, LANES), F32)],
        scratch_shapes=[pltpu.VMEM((1, LANES), F32)],
        compiler_params=_params("arbitrary"),
    )(route, counts_in, tri)


def _dispatch(e_all, rank_all, counts, n_blocks):
    blk = MOE_BLOCK
    nk = e_all.shape[0] * TOP_K
    nblk_e = (counts + blk - 1) // blk
    bend = jnp.cumsum(nblk_e)
    bstart = bend - nblk_e
    onehot = e_all[:, :, None] == jnp.arange(N_EXPERTS, dtype=jnp.int32)[None, None, :]
    dest = rank_all + jnp.sum(jnp.where(onehot, (bstart * blk)[None, None, :], 0), axis=-1)
    dest = dest.reshape(nk).astype(jnp.int32)
    n_used = bend[-1].astype(jnp.int32)
    bidx = jnp.arange(n_blocks, dtype=jnp.int32)
    blk_e = jnp.minimum(jnp.sum((bend[None, :] <= bidx[:, None]).astype(jnp.int32), axis=1), N_EXPERTS - 1)
    eq = blk_e[:, None] == jnp.arange(N_EXPERTS, dtype=jnp.int32)[None, :]
    cnt_b = jnp.sum(jnp.where(eq, counts[None, :], 0), axis=1)
    start_b = jnp.sum(jnp.where(eq, bstart[None, :], 0), axis=1)
    n_valid = jnp.clip(cnt_b - (bidx - start_b) * blk, 0, blk)
    n_valid = jnp.where(bidx < n_used, n_valid, 0).astype(jnp.int32)
    last_e = jnp.sum(jnp.where(bidx == n_used - 1, blk_e, 0))
    blk_e = jnp.where(bidx < n_used, blk_e, last_e).astype(jnp.int32)
    return blk_e, n_valid, n_used.reshape(1), dest


def _for_rows(n, fn):
    n_main = n // DMA_UNROLL

    def main(t, _):
        for u in range(DMA_UNROLL):
            fn(t * DMA_UNROLL + u)
        return 0

    def rest(r, _):
        fn(r)
        return 0

    lax.fori_loop(0, n_main, main, 0)
    lax.fori_loop(n_main * DMA_UNROLL, n, rest, 0)


def _moe_kernel(be_ref, nv_ref, nu_ref, dest_ref, xn_hbm, wg_ref, wu_ref, wd_ref, y_hbm,
                xbuf, ybuf, gsem, ssem, wgb, wub, wdb, src_ref, *, blk, n_chunks):
    i = pl.program_id(0)
    n_used = nu_ref[0]
    slot = i % 2

    @pl.when(i == 0)
    def _():
        def put(a):
            src_ref[dest_ref[a]] = a
        _for_rows(dest_ref.shape[0], put)

    def rows(t):
        return pl.ds(pl.multiple_of(t * n_chunks, n_chunks), n_chunks)

    def gather_copy(bi, sl, r):
        tok = lax.shift_right_logical(src_ref[bi * blk + r], TOP_K.bit_length() - 1)
        return pltpu.make_async_copy(xn_hbm.at[rows(tok)], xbuf.at[sl, rows(r)], gsem.at[sl])

    def scatter_copy(bi, sl, r):
        return pltpu.make_async_copy(ybuf.at[sl, rows(r)], y_hbm.at[rows(src_ref[bi * blk + r])], ssem.at[sl])

    def start_all(make, bi, sl):
        _for_rows(nv_ref[bi], lambda r: make(bi, sl, r).start())

    def wait_all(make, bi, sl):
        _for_rows(nv_ref[bi], lambda r: make(bi, sl, 0).wait())

    @pl.when(i == 0)
    def _():
        xbuf[...] = jnp.zeros_like(xbuf)

        @pl.when(n_used > 0)
        def _():
            start_all(gather_copy, 0, 0)

    @pl.when(i + 1 < n_used)
    def _():
        start_all(gather_copy, i + 1, 1 - slot)

    @pl.when(i < n_used)
    def _():
        @pl.when((i == 0) | (be_ref[i] != be_ref[jnp.maximum(i - 1, 0)]))
        def _():
            wgb[...] = wg_ref[...].astype(BF16)
            wub[...] = wu_ref[...].astype(BF16)
            wdb[...] = wd_ref[...].astype(BF16)

        wait_all(gather_copy, i, slot)
        x = jnp.concatenate([xbuf[slot, pl.ds(ch, blk, stride=n_chunks), :] for ch in range(n_chunks)],
                            axis=1).astype(BF16)
        g = _dot(x, wgb[...])
        u = _dot(x, wub[...])
        hmid = (g * _sigmoid(g) * u).astype(BF16)
        y = _dot(hmid, wdb[...])

        @pl.when(i >= 2)
        def _():
            wait_all(scatter_copy, i - 2, slot)

        for ch in range(n_chunks):
            ybuf[slot, pl.ds(ch, blk, stride=n_chunks), :] = y[:, ch * LANES:(ch + 1) * LANES]
        start_all(scatter_copy, i, slot)

        @pl.when(i == n_used - 1)
        def _():
            @pl.when(i >= 1)
            def _():
                wait_all(scatter_copy, i - 1, 1 - slot)

            wait_all(scatter_copy, i, slot)


def _moe_experts(blk_e, n_valid, n_used, dest, xn_all, w_gate, w_up, w_down, *, layer, n_blocks):
    d = w_gate.shape[-2]
    n_chunks = d // LANES
    n_all = xn_all.shape[0] // n_chunks
    de = w_gate.shape[-1]
    blk = MOE_BLOCK
    wspec = lambda r, c_: pl.BlockSpec((None, None, r, c_), lambda i, be, nv, nu, s: (layer, be[i], 0, 0))
    grid_spec = pltpu.PrefetchScalarGridSpec(
        num_scalar_prefetch=4, grid=(n_blocks,),
        in_specs=[pl.BlockSpec(memory_space=pl.ANY), wspec(d, de), wspec(d, de), wspec(de, d)],
        out_specs=pl.BlockSpec(memory_space=pl.ANY),
        scratch_shapes=[pltpu.VMEM((2, blk * n_chunks, LANES), F32), pltpu.VMEM((2, blk * n_chunks, LANES), F32),
                        pltpu.SemaphoreType.DMA((2,)), pltpu.SemaphoreType.DMA((2,)),
                        pltpu.VMEM((d, de), BF16), pltpu.VMEM((d, de), BF16), pltpu.VMEM((de, d), BF16),
                        pltpu.SMEM((n_blocks * blk,), jnp.int32)])
    return pl.pallas_call(
        functools.partial(_moe_kernel, blk=blk, n_chunks=n_chunks), grid_spec=grid_spec, name="moe_experts",
        out_shape=jax.ShapeDtypeStruct((n_all * TOP_K * n_chunks, LANES), F32),
        compiler_params=_params("arbitrary"),
    )(blk_e, n_valid, n_used, dest, xn_all, w_gate, w_up, w_down)


def _final_kernel(x2_ref, y_ref, route_ref, g_ref, o_ref):
    o_ref[...] = _rms(_moe_combine(x2_ref[...], route_ref[...], y_ref), g_ref[...])


def _final(x2, y, route, g, *, tm, y_block_offset):
    n, d = x2.shape
    return pl.pallas_call(
        _final_kernel, grid=(n // tm,), name="final",
        in_specs=[pl.BlockSpec((tm, d), lambda i: (i, 0)),
                  pl.BlockSpec((tm * TOP_K * d // LANES, LANES), lambda i: (y_block_offset + i, 0)),
                  pl.BlockSpec((tm, LANES), lambda i: (i, 0)), pl.BlockSpec((1, d), lambda i: (0, 0))],
        out_specs=pl.BlockSpec((tm, d), lambda i: (i, 0)),
        out_shape=jax.ShapeDtypeStruct((n, d), F32),
        compiler_params=_params("parallel"),
    )(x2, y, route, g)


def _block_diag(w):
    n, a, b = w.shape
    out = jnp.zeros((n * a, n * b), w.dtype)
    for i in range(n):
        out = out.at[i * a:(i + 1) * a, i * b:(i + 1) * b].set(w[i])
    return out


def _pad_lanes(v, n=LANES):
    return jnp.zeros((1, n), F32).at[0, :v.shape[0]].set(v)


def _tile_for(t, pref):
    tm = min(pref, t)
    while t % tm:
        tm //= 2
    return tm


def kernel(x_prompt, x_sample, cache_k, cache_v, cache_logf, state_conv, state_rglru, state_pool, page_table, norm_mix, w_in, b_forget, conv_w, conv_b, w_gate_a, b_gate_a, w_gate_x, b_gate_x, lru_lambda, w_pool, pool_scale, norm_groups, w_out, norm_ffn, w_router_group, b_router_group, w_router_expert, b_router_expert, w_up, w_gate, w_down, norm_final):
    depth = w_in.shape[0]
    b_p, t_p, d = x_prompt.shape
    b_s = x_sample.shape[0]
    n_heads = b_forget.shape[1]
    d_att = n_heads * HEAD_DIM
    c = conv_w.shape[-1]
    n_p = b_p * t_p
    n_all = n_p + b_s
    n_pages, page = page_table.shape[1], cache_k.shape[2]
    past_len = n_pages * page
    nk = n_all * TOP_K
    n_blocks = -(-(nk + N_EXPERTS * (MOE_BLOCK - 1)) // MOE_BLOCK)

    tm_p = _tile_for(t_p, 512)
    tk = _tile_for(t_p, 512)
    n_chains = 2 if t_p % (2 * tk) == 0 else 1
    tr = _tile_for(t_p, 256)
    tm_r = _tile_for(n_p, 1024)
    tm_f = _tile_for(n_p, 512)

    cache_kt = cache_k.transpose(0, 1, 3, 4, 2)
    cache_vt = cache_v.transpose(0, 1, 3, 4, 2)
    x_p = x_prompt
    x_s = x_sample.reshape(1, b_s, d)
    moe_p = moe_s = None
    off_f = 3 * d_att
    outs = [[] for _ in range(12)]
    for l in range(depth):
        wl = w_in[l]
        w_cat = jnp.concatenate(
            [wl[:, :off_f], wl[:, off_f + n_heads:], wl[:, off_f:off_f + n_heads],
             jnp.zeros((d, LANES - n_heads), F32)], axis=1).astype(BF16)
        g_mix = norm_mix[l].reshape(1, d)
        b_pad = _pad_lanes(b_forget[l])
        mw = (conv_w[l], conv_b[l].reshape(1, c), _block_diag(w_gate_a[l]).astype(BF16),
              b_gate_a[l].reshape(1, c), _block_diag(w_gate_x[l]).astype(BF16), b_gate_x[l].reshape(1, c),
              lru_lambda[l].reshape(1, c), _block_diag(w_pool[l]).astype(BF16), pool_scale[l].reshape(1, c))
        ng, wo = norm_groups[l], w_out[l]
        ga_pad = jnp.zeros((n_heads, 1, LANES), F32).at[:, 0, :HEAD_DIM].set(ng[:d_att].reshape(n_heads, HEAD_DIM))
        woa_pad = jnp.zeros((n_heads, LANES, d), F32).at[:, :HEAD_DIM].set(
            wo[:d_att].reshape(n_heads, HEAD_DIM, d)).astype(BF16)
        w_r = jnp.zeros((d, LANES), F32).at[:, :N_GROUPS].set(w_router_group[l]).at[
            :, N_GROUPS:N_GROUPS + N_EXPERTS].set(w_router_expert[l])
        wr_hi = w_r.astype(BF16)
        wr_lo = (w_r - wr_hi.astype(F32)).astype(BF16)
        b_r = _pad_lanes(jnp.concatenate([b_router_group[l], b_router_expert[l]]))
        ow_rest = (wo[d_att:].astype(BF16), norm_ffn[l].reshape(1, d), wr_hi, wr_lo, b_r)
        g_rest = ng[d_att:].reshape(1, 2 * c)
        ow_p = (ga_pad, g_rest, woa_pad, *ow_rest)
        ow_s = (ng[:d_att].reshape(1, d_att), g_rest, wo[:d_att].astype(BF16), *ow_rest)

        res = _inproj(x_p, g_mix, w_cat, tm=tm_p, head_layout=True, moe=moe_p, b_pad=b_pad)
        if moe_p is not None:
            x_p, res = res[0], res[1:]
        qa, ka, vp, k_p, v_p, lf_p, rest_p = res
        o_p = _attention(qa, ka, vp, tk=tk, n_chains=n_chains)
        y_mix_p, conv_t, h_t, pool_t = _mix_prompt(rest_p, mw, tr=tr)
        x2_p, xn2_p, route_p = _outproj(x_p, o_p, y_mix_p, ow_p, tm=tm_p, att_padded=True)

        res = _inproj(x_s, g_mix, w_cat, tm=b_s, head_layout=False, moe=moe_s, y_block_offset=n_p // b_s)
        if moe_s is not None:
            x_s, res = res[0], res[1:]
        q_s, k_s, v_s, f_s, rest_s = res
        heads = lambda a: a.reshape(b_s, n_heads, HEAD_DIM)
        lfp_t = cache_logf[l][page_table].reshape(b_s, past_len, n_heads).transpose(0, 2, 1)
        o_s, lf_s = _decode_attention(page_table, heads(q_s), heads(k_s), heads(v_s), f_s[0], b_pad, lfp_t,
                                      cache_kt, cache_vt, layer=l)
        y_mix_s, conv_s, h_s, pool_s = _mix_sample(
            rest_s[0], state_conv[l].transpose(1, 0, 2), state_rglru[l], state_pool[l].transpose(1, 0, 2),
            mw, pos0=past_len)
        x2_s, xn2_s, route_s = _outproj(x_s, o_s.reshape(1, b_s, d_att), y_mix_s[None], ow_s, tm=b_s,
                                        att_padded=False)

        route_p2, route_s2 = route_p.reshape(n_p, LANES), route_s.reshape(b_s, LANES)
        rank_p, cnt_p = _rank(route_p2, jnp.zeros((1, LANES), F32), tm=tm_r)
        rank_s, cnt = _rank(route_s2, cnt_p, tm=b_s)
        first2 = lambda a_p, a_s: jnp.concatenate([a_p[:, :TOP_K], a_s[:, :TOP_K]], axis=0).astype(jnp.int32)
        blk_e, n_valid, n_used, dest = _dispatch(first2(route_p2, route_s2), first2(rank_p, rank_s),
                                                cnt[0, :N_EXPERTS].astype(jnp.int32), n_blocks)
        xn2_all = jnp.concatenate([xn2_p.reshape(-1, LANES), xn2_s.reshape(-1, LANES)], axis=0)
        y_moe = _moe_experts(blk_e, n_valid, n_used, dest, xn2_all, w_gate, w_up, w_down,
                             layer=l, n_blocks=n_blocks)
        x_p, x_s = x2_p, x2_s
        moe_p, moe_s = (y_moe, route_p), (y_moe, route_s)

        per_layer = (
            k_p.reshape(b_p, t_p, n_heads, HEAD_DIM), v_p.reshape(b_p, t_p, n_heads, HEAD_DIM),
            lf_p[:, :, :n_heads], conv_t[:, 8 - (CONV_W - 1):], h_t[:, 7], pool_t[:, 16 - POOL_BUF:],
            k_s.reshape(b_s, 1, n_heads, HEAD_DIM), v_s.reshape(b_s, 1, n_heads, HEAD_DIM),
            lf_s[:, None, :n_heads], conv_s.transpose(1, 0, 2), h_s, pool_s.transpose(1, 0, 2))
        for acc, val in zip(outs, per_layer):
            acc.append(val)
    g_fin = norm_final.reshape(1, d)
    y_prompt = _final(x_p.reshape(n_p, d), moe_p[0], moe_p[1].reshape(n_p, LANES), g_fin,
                      tm=tm_f, y_block_offset=0).reshape(b_p, t_p, d)
    y_sample = _final(x_s.reshape(b_s, d), moe_s[0], moe_s[1].reshape(b_s, LANES), g_fin,
                      tm=b_s, y_block_offset=n_p // b_s).reshape(b_s, 1, d)
    return (y_prompt, y_sample, *[jnp.stack(o) for o in outs])
```

```python
import functools

import jax
import jax.numpy as jnp
from jax import lax
from jax.experimental import pallas as pl
from jax.experimental.pallas import tpu as pltpu

F32 = jnp.float32
BF16 = jnp.bfloat16

EPS = 1e-6
HEAD_DIM = 64
LANES = 128
RG_C = 8.0
CONV_W = 4
POOL_WINDOWS = (2, 4, 8, 16)
POOL_BUF = max(POOL_WINDOWS) - 1
N_GROUPS = 4
EXPERTS_PER_GROUP = 8
N_EXPERTS = N_GROUPS * EXPERTS_PER_GROUP
TOP_K = 2
MOE_BLOCK = 256
DMA_UNROLL = 8
NEG = -1e30
LOG2E = 1.4426950408889634
N_BIAS = 3
VMEM_LIMIT = 56 * 1024 * 1024


def _params(*sem):
    return pltpu.CompilerParams(dimension_semantics=sem, vmem_limit_bytes=VMEM_LIMIT)


def _split3(x):
    hi = x.astype(BF16)
    r1 = x - hi.astype(F32)
    mid = r1.astype(BF16)
    lo = (r1 - mid.astype(F32)).astype(BF16)
    return hi, mid, lo


def _dot(a, b):
    return jnp.dot(a, b, preferred_element_type=F32)


def _dot_nt(a, b):
    return lax.dot_general(a, b, (((1,), (1,)), ((), ())), preferred_element_type=F32)


def _dot3(x, w_bf16):
    hi, mid, lo = _split3(x)
    return _dot(hi, w_bf16) + _dot(mid, w_bf16) + _dot(lo, w_bf16)


def _dot3_left(w_bf16, x):
    hi, mid, lo = _split3(x)
    return _dot(w_bf16, hi) + _dot(w_bf16, mid) + _dot(w_bf16, lo)


def _sigmoid(x):
    return 1.0 / (1.0 + jnp.exp(-x))


def _softplus(x):
    return jnp.maximum(x, 0.0) + jnp.log1p(jnp.exp(-jnp.abs(x)))


def _gelu_tanh(x):
    return 0.5 * x * (1.0 + jnp.tanh(0.7978845608028654 * (x + 0.044715 * (x * x * x))))


def _rms(x, g):
    return x * lax.rsqrt(jnp.mean(x * x, axis=-1, keepdims=True) + EPS) * g


def _moe_combine(x2, route, y_ref):
    tm, d = x2.shape
    n_chunks = d // LANES
    g1, g2 = route[:, 2:3], route[:, 3:4]
    step = TOP_K * n_chunks
    out = []
    for c in range(n_chunks):
        y0 = y_ref[pl.ds(c, tm, stride=step), :]
        y1 = y_ref[pl.ds(n_chunks + c, tm, stride=step), :]
        out.append(x2[:, c * LANES:(c + 1) * LANES] + (g1 * y0 + g2 * y1))
    return jnp.concatenate(out, axis=1)


def _inproj_kernel(*refs, d_att, d_rest, head_layout, has_moe, scale):
    refs = list(refs)
    x_ref = refs.pop(0)
    if has_moe:
        y_ref, route_ref = refs.pop(0), refs.pop(0)
    g_ref, w_ref = refs.pop(0), refs.pop(0)
    if head_layout:
        b_ref, tri_ref, place_ref = refs.pop(0), refs.pop(0), refs.pop(0)
    x = x_ref[...]
    if has_moe:
        x = _moe_combine(x, route_ref[...], y_ref)
        refs.pop(0)[...] = x
    xn = _rms(x, g_ref[...]).astype(BF16)

    def sec(lo, n):
        return _dot(xn, w_ref[:, lo:lo + n])

    q = sec(0, d_att) * scale
    k = sec(d_att, d_att)
    v = sec(2 * d_att, d_att)
    rest = sec(3 * d_att, d_rest)
    f = sec(3 * d_att + d_rest, LANES)
    if not head_layout:
        q_ref, k_ref, v_ref, f_ref, r_ref = refs
        q_ref[...] = q
        k_ref[...] = k
        v_ref[...] = v
        f_ref[...] = f
        r_ref[...] = rest
        return
    qa_ref, ka_ref, vp_ref, k_ref, v_ref, lf_ref, r_ref, carry_ref = refs
    k_ref[...] = k.T
    v_ref[...] = v.T
    r_ref[...] = rest

    @pl.when(pl.program_id(1) == 0)
    def _():
        carry_ref[...] = jnp.zeros_like(carry_ref)

    lf = -_softplus(-(f + b_ref[...]))
    lf_ref[...] = lf
    cs = _dot3_left(tri_ref[...], lf) + carry_ref[...]
    carry_ref[...] = carry_ref[...] + jnp.sum(lf, axis=0, keepdims=True)

    tm = q.shape[0]
    n_heads = d_att // HEAD_DIM
    lane = lax.broadcasted_iota(jnp.int32, (tm, LANES), 1)
    parts = [p.astype(F32) for p in _split3(cs * (-LOG2E))]
    packed = jnp.where(lane < n_heads, parts[0], 0.0)
    for j in range(1, N_BIAS):
        in_j = (lane >= j * n_heads) & (lane < (j + 1) * n_heads)
        packed = packed + jnp.where(in_j, pltpu.roll(parts[j], j * n_heads, axis=1), 0.0)
    aug = _dot(packed.astype(BF16), place_ref[...])

    low = lane < HEAD_DIM
    q_fill = jnp.where((lane >= HEAD_DIM) & (lane < HEAD_DIM + N_BIAS), 1.0, 0.0)
    v_fill = jnp.where(lane == HEAD_DIM, 1.0, 0.0)
    for j in range(d_att // LANES):
        sl = slice(j * LANES, (j + 1) * LANES)
        for val, ref, fills in ((q, qa_ref, (q_fill, q_fill)), (v, vp_ref, (v_fill, v_fill)),
                                (k, ka_ref, (aug[:, 2 * j * LANES:(2 * j + 1) * LANES],
                                             aug[:, (2 * j + 1) * LANES:(2 * j + 2) * LANES]))):
            pair = val[:, sl]
            ref[2 * j] = jnp.where(low, pair, fills[0]).astype(BF16)
            ref[2 * j + 1] = jnp.where(low, pltpu.roll(pair, HEAD_DIM, axis=1), fills[1]).astype(BF16)


def _bias_placement(n_heads):
    rows = jnp.arange(LANES)[:, None]
    cols = jnp.arange(n_heads * LANES)[None, :]
    j, h = rows // n_heads, rows % n_heads
    hit = (rows < N_BIAS * n_heads) & (cols == h * LANES + HEAD_DIM + j)
    return hit.astype(BF16)


def _inproj(x, g, w, *, tm, head_layout, moe=None, b_pad=None, y_block_offset=0):
    b, t, d = x.shape
    n_w = w.shape[1]
    d_att = 512
    d_rest = n_w - 3 * d_att - LANES
    n_heads = d_att // HEAD_DIM
    nt = t // tm
    row = lambda n: pl.BlockSpec((None, tm, n), lambda i, j: (i, j, 0))
    const = lambda a: pl.BlockSpec(a.shape, lambda i, j: (0,) * a.ndim)
    slab = pl.BlockSpec((None, n_heads, tm, LANES), lambda i, j: (i, 0, j, 0))
    f32s = lambda n: jax.ShapeDtypeStruct((b, t, n), F32)
    args, in_specs = [x], [row(d)]
    out_shape, out_specs = [], []
    if moe is not None:
        y, route = moe
        args += [y, route]
        in_specs += [pl.BlockSpec((tm * TOP_K * d // LANES, LANES), lambda i, j: (y_block_offset + i * nt + j, 0)),
                     row(LANES)]
        out_shape.append(f32s(d))
        out_specs.append(row(d))
    args += [g, w]
    in_specs += [const(g), const(w)]
    scratch = []
    if head_layout:
        tri = (jnp.arange(tm)[:, None] >= jnp.arange(tm)[None, :]).astype(BF16)
        place = _bias_placement(n_heads)
        args += [b_pad, tri, place]
        in_specs += [const(b_pad), const(tri), const(place)]
        slab_s = jax.ShapeDtypeStruct((b, n_heads, t, LANES), BF16)
        out_shape += [slab_s, slab_s, slab_s]
        out_specs += [slab, slab, slab]
        scratch = [pltpu.VMEM((1, LANES), F32)]
        scale = HEAD_DIM ** -0.5 * LOG2E
    else:
        out_shape.append(f32s(d_att))
        out_specs.append(row(d_att))
        scale = HEAD_DIM ** -0.5
    if head_layout:
        kt_s = jax.ShapeDtypeStruct((b, d_att, t), F32)
        kt_spec = pl.BlockSpec((None, d_att, tm), lambda i, j: (i, 0, j))
        out_shape += [kt_s, kt_s, f32s(LANES), f32s(d_rest)]
        out_specs += [kt_spec, kt_spec, row(LANES), row(d_rest)]
    else:
        out_shape += [f32s(d_att), f32s(d_att), f32s(LANES), f32s(d_rest)]
        out_specs += [row(d_att), row(d_att), row(LANES), row(d_rest)]
    kern = functools.partial(_inproj_kernel, d_att=d_att, d_rest=d_rest, head_layout=head_layout,
                             has_moe=moe is not None, scale=scale)
    return pl.pallas_call(
        kern, grid=(b, nt), name="inproj_heads" if head_layout else "inproj_flat",
        in_specs=in_specs, out_specs=out_specs, out_shape=out_shape, scratch_shapes=scratch,
        compiler_params=_params("parallel", "arbitrary"),
    )(*args)


def _attention_kernel(q_ref, k_ref, v_ref, o_ref, *, tk, n_chains):
    qi = pl.program_id(2)
    qs = [q_ref[c * tk:(c + 1) * tk, :] for c in range(n_chains)]

    def kv(j):
        off = pl.multiple_of(j * tk, tk)
        return k_ref[pl.ds(off, tk), :], v_ref[pl.ds(off, tk), :]

    def tile(q, k, v, m, acc, masked):
        s = _dot_nt(q, k)
        if masked:
            row = lax.broadcasted_iota(jnp.int32, (tk, tk), 0)
            col = lax.broadcasted_iota(jnp.int32, (tk, tk), 1)
            s = jnp.where(row >= col, s, NEG)
        m_new = jnp.maximum(m, jnp.max(s, axis=-1, keepdims=True))
        p = jnp.exp2(s - m_new)
        acc = jnp.exp2(m - m_new) * acc + _dot(p.astype(BF16), v)
        return m_new, acc

    def trip(n_tiles):
        def body(j, carry):
            first, carry = carry
            for d in range(n_tiles):
                k, v = kv(first + j * n_tiles + d)
                carry = tuple(tile(qs[c], k, v, *carry[c], False) for c in range(n_chains))
            return first, carry
        return body

    init = tuple((jnp.full((tk, 1), NEG, F32), jnp.zeros((tk, LANES), F32)) for _ in range(n_chains))
    n_full = qi * n_chains
    long_trip = 2 * n_chains
    n_long = n_full // long_trip
    _, carry = lax.fori_loop(0, n_long, trip(long_trip), (0, init))
    _, carry = lax.fori_loop(0, (n_full - n_long * long_trip) // n_chains, trip(n_chains),
                             (n_long * long_trip, carry))
    carry = list(carry)
    for d in range(n_chains):
        k, v = kv(qi * n_chains + d)
        for c in range(d, n_chains):
            carry[c] = tile(qs[c], k, v, *carry[c], c == d)
    for c in range(n_chains):
        acc = carry[c][1]
        o_ref[c * tk:(c + 1) * tk, :] = acc / acc[:, HEAD_DIM:HEAD_DIM + 1]


def _attention(qa, ka, vp, *, tk, n_chains):
    b, h, t, _ = qa.shape
    tq = tk * n_chains
    tile = pl.BlockSpec((None, None, tq, LANES), lambda i, j, n: (i, j, n, 0))
    full = pl.BlockSpec((None, None, t, LANES), lambda i, j, n: (i, j, 0, 0))
    return pl.pallas_call(
        functools.partial(_attention_kernel, tk=tk, n_chains=n_chains), grid=(b, h, t // tq), name="attention",
        in_specs=[tile, full, full], out_specs=tile,
        out_shape=jax.ShapeDtypeStruct((b, h, t, LANES), F32),
        compiler_params=_params("parallel", "parallel", "arbitrary"),
    )(qa, ka, vp)


def _rglru_gates(xc, gb, wa_ref, ba_ref, wx_ref, bx_ref, lam_ref):
    xcb = xc.astype(BF16)
    r = _sigmoid(_dot(xcb, wa_ref[...]) + ba_ref[...])
    i = _sigmoid(_dot(xcb, wx_ref[...]) + bx_ref[...])
    log_a = -RG_C * r * _softplus(-lam_ref[...])
    a = jnp.exp(log_a)
    u = jnp.sqrt(jnp.tanh(-log_a) * (a * a + 1.0)) * (i * xc)
    return a, u, _gelu_tanh(gb)


def _pool_lane_select(c, w2, w4, w8, w16):
    lane = lax.broadcasted_iota(jnp.int32, c.shape, 1)
    gw = c.shape[1] // len(POOL_WINDOWS)
    return jnp.where(lane < gw, w2, jnp.where(lane < 2 * gw, w4, jnp.where(lane < 3 * gw, w8, w16)))


def _mix_prompt_kernel(r_ref, cw_ref, cb_ref, wa_ref, ba_ref, wx_ref, bx_ref, lam_ref, wp_ref, ps_ref,
                       y_ref, conv_ref, h_ref, pool_ref, xprev_ref, uprev_ref, htail_ref, *, tr, c):
    t = pl.program_id(1)

    @pl.when(t == 0)
    def _():
        xprev_ref[...] = jnp.zeros_like(xprev_ref)
        uprev_ref[...] = jnp.zeros_like(uprev_ref)
        htail_ref[...] = jnp.zeros_like(htail_ref)

    xb = r_ref[:, 0:c]
    gb = r_ref[:, c:2 * c]
    u_in = r_ref[:, 2 * c:3 * c]

    ext = jnp.concatenate([xprev_ref[...], xb], axis=0)
    xc = cb_ref[...] + cw_ref[CONV_W - 1:CONV_W, :] * xb
    for j in range(1, CONV_W):
        xc = xc + cw_ref[CONV_W - 1 - j:CONV_W - j, :] * pltpu.roll(ext, j, axis=0)[8:]

    a, u, gate = _rglru_gates(xc, gb, wa_ref, ba_ref, wx_ref, bx_ref, lam_ref)

    row = lax.broadcasted_iota(jnp.int32, (tr, c), 0)
    big_a, big_u = a, u
    sh = 1
    while sh < tr:
        keep = row >= sh
        a_s = jnp.where(keep, pltpu.roll(big_a, sh, axis=0), 1.0)
        u_s = jnp.where(keep, pltpu.roll(big_u, sh, axis=0), 0.0)
        big_u = big_a * u_s + big_u
        big_a = big_a * a_s
        sh *= 2
    h = big_u + big_a * htail_ref[7:8, :]

    extu = jnp.concatenate([uprev_ref[...], u_in], axis=0)
    s2 = extu + pltpu.roll(extu, 1, axis=0)
    s4 = s2 + pltpu.roll(s2, 2, axis=0)
    s8 = s4 + pltpu.roll(s4, 4, axis=0)
    s16 = s8 + pltpu.roll(s8, 8, axis=0)
    win = _pool_lane_select(u_in, s2[16:], s4[16:], s8[16:], s16[16:])
    wsz = _pool_lane_select(u_in, 2.0, 4.0, 8.0, 16.0)
    pos = (t * tr + row).astype(F32)
    d = win / jnp.minimum(pos + 1.0, wsz) - u_in
    y_pool = _dot(d.astype(BF16), wp_ref[...]) * ps_ref[...]

    y_ref[:, 0:c] = h * gate
    y_ref[:, c:2 * c] = y_pool
    conv_ref[...] = xb[tr - 8:]
    h_ref[...] = h[tr - 8:]
    pool_ref[...] = u_in[tr - 16:]
    xprev_ref[...] = xb[tr - 8:]
    uprev_ref[...] = u_in[tr - 16:]
    htail_ref[...] = h[tr - 8:]


def _mix_prompt(rest, mw, *, tr):
    b, t, c3 = rest.shape
    c = c3 // 3
    vec = lambda n: pl.BlockSpec((n, c), lambda i, j: (0, 0))
    mat = pl.BlockSpec((c, c), lambda i, j: (0, 0))
    tail = lambda n: pl.BlockSpec((None, n, c), lambda i, j: (i, 0, 0))
    return pl.pallas_call(
        functools.partial(_mix_prompt_kernel, tr=tr, c=c), grid=(b, t // tr), name="mix_prompt",
        in_specs=[pl.BlockSpec((None, tr, c3), lambda i, j: (i, j, 0)),
                  vec(CONV_W), vec(1), mat, vec(1), mat, vec(1), vec(1), mat, vec(1)],
        out_specs=[pl.BlockSpec((None, tr, 2 * c), lambda i, j: (i, j, 0)), tail(8), tail(8), tail(16)],
        out_shape=[jax.ShapeDtypeStruct((b, t, 2 * c), F32), jax.ShapeDtypeStruct((b, 8, c), F32),
                   jax.ShapeDtypeStruct((b, 8, c), F32), jax.ShapeDtypeStruct((b, 16, c), F32)],
        scratch_shapes=[pltpu.VMEM((8, c), F32), pltpu.VMEM((16, c), F32), pltpu.VMEM((8, c), F32)],
        compiler_params=_params("parallel", "arbitrary"),
    )(rest, *mw)


def _mix_sample_kernel(r_ref, sc_ref, h0_ref, sp_ref, cw_ref, cb_ref, wa_ref, ba_ref, wx_ref, bx_ref,
                       lam_ref, wp_ref, ps_ref, y_ref, conv_ref, h_ref, pool_ref, *, c, pos0):
    xb = r_ref[:, 0:c]
    gb = r_ref[:, c:2 * c]
    u_in = r_ref[:, 2 * c:3 * c]

    xc = cb_ref[...] + cw_ref[CONV_W - 1:CONV_W, :] * xb
    for j in range(CONV_W - 1):
        xc = xc + cw_ref[j:j + 1, :] * sc_ref[j]
    a, u, gate = _rglru_gates(xc, gb, wa_ref, ba_ref, wx_ref, bx_ref, lam_ref)
    h = a * h0_ref[...] + u

    sums = {}
    acc = u_in
    for n in range(1, POOL_BUF + 1):
        acc = acc + sp_ref[POOL_BUF - n]
        if n + 1 in POOL_WINDOWS:
            sums[n + 1] = acc
    win = _pool_lane_select(u_in, *[sums[w] for w in POOL_WINDOWS])
    cnt = _pool_lane_select(u_in, *[float(min(pos0 + 1, w)) for w in POOL_WINDOWS])
    d = win / cnt - u_in
    y_pool = _dot(d.astype(BF16), wp_ref[...]) * ps_ref[...]

    y_ref[:, 0:c] = h * gate
    y_ref[:, c:2 * c] = y_pool
    h_ref[...] = h
    for j in range(CONV_W - 2):
        conv_ref[j] = sc_ref[j + 1]
    conv_ref[CONV_W - 2] = xb
    for j in range(POOL_BUF - 1):
        pool_ref[j] = sp_ref[j + 1]
    pool_ref[POOL_BUF - 1] = u_in


def _mix_sample(rest, sc_t, h0, sp_t, mw, *, pos0):
    bs, c3 = rest.shape
    c = c3 // 3
    return pl.pallas_call(
        functools.partial(_mix_sample_kernel, c=c, pos0=pos0), name="mix_sample",
        out_shape=[jax.ShapeDtypeStruct((bs, 2 * c), F32), jax.ShapeDtypeStruct(sc_t.shape, F32),
                   jax.ShapeDtypeStruct((bs, c), F32), jax.ShapeDtypeStruct(sp_t.shape, F32)],
        compiler_params=pltpu.CompilerParams(vmem_limit_bytes=VMEM_LIMIT),
    )(rest, sc_t, h0, sp_t, *mw)


def _decode_kernel(pt_ref, q_ref, kn_ref, vn_ref, f_ref, b_ref, lfp_ref, tri_ref, ck_hbm, cv_hbm,
                   o_ref, lfn_ref, kbuf, vbuf, ksem, vsem, *, layer, n_pages, page, n_heads):
    b = pl.program_id(0)
    nb = pl.num_programs(0)

    def copies(seq, slot):
        out = []
        for j in range(n_pages):
            pid = pt_ref[seq * n_pages + j]
            out.append(pltpu.make_async_copy(ck_hbm.at[layer, pid], kbuf.at[slot, j], ksem.at[slot]))
            out.append(pltpu.make_async_copy(cv_hbm.at[layer, pid], vbuf.at[slot, j], vsem.at[slot]))
        return out

    slot = b % 2

    @pl.when(b == 0)
    def _():
        for cp in copies(0, 0):
            cp.start()

    @pl.when(b + 1 < nb)
    def _():
        for cp in copies(b + 1, 1 - slot):
            cp.start()

    lfn = -_softplus(-(f_ref[...] + b_ref[...]))
    lfn_ref[...] = lfn
    sub8 = lax.broadcasted_iota(jnp.int32, (n_heads, LANES), 0)
    lane8 = lax.broadcasted_iota(jnp.int32, (n_heads, LANES), 1)
    lfn_col = jnp.sum(jnp.where(sub8 == lane8, jnp.broadcast_to(lfn, (n_heads, LANES)), 0.0),
                      axis=-1, keepdims=True)

    pages = [lfp_ref[:, p * page:(p + 1) * page] for p in range(n_pages)]
    within = _dot3(jnp.concatenate(pages, axis=0), tri_ref[...])
    blocks = [None] * n_pages
    carry = lfn_col
    for p in reversed(range(n_pages)):
        blocks[p] = within[p * n_heads:(p + 1) * n_heads] + carry
        carry = carry + jnp.sum(pages[p], axis=-1, keepdims=True)
    bias = jnp.concatenate(blocks, axis=1)

    q = q_ref[...]
    s_new = jnp.sum(q * kn_ref[...], axis=-1, keepdims=True)
    q16 = jnp.concatenate([q, jnp.zeros_like(q)], axis=0).astype(BF16)
    p_len = n_pages * page
    head_row = lax.broadcasted_iota(jnp.int32, (n_heads, p_len), 0)

    for cp in copies(b, slot):
        cp.wait()

    def head_mat(buf, h):
        return jnp.concatenate([buf[slot, j, h] for j in range(n_pages)], axis=1).astype(BF16)

    s = bias
    for h in range(n_heads):
        s_h = _dot(q16, head_mat(kbuf, h))[0:n_heads]
        s = s + jnp.where(head_row == h, s_h, 0.0)
    m = jnp.maximum(jnp.max(s, axis=-1, keepdims=True), s_new)
    p_past = jnp.exp(s - m)
    p_new = jnp.exp(s_new - m)
    z = jnp.sum(p_past, axis=-1, keepdims=True) + p_new
    p16 = jnp.concatenate([p_past, jnp.zeros_like(p_past)], axis=0).astype(BF16)
    out_row = lax.broadcasted_iota(jnp.int32, (n_heads, HEAD_DIM), 0)
    o = p_new * vn_ref[...]
    for h in range(n_heads):
        o_h = _dot_nt(p16, head_mat(vbuf, h))[0:n_heads]
        o = o + jnp.where(out_row == h, o_h, 0.0)
    o_ref[...] = o / z


def _decode_attention(page_table, q, k_new, v_new, f_raw, b_pad, lfp_t, cache_kt, cache_vt, *, layer):
    bs, n_heads, _ = q.shape
    n_pages = page_table.shape[1]
    page = cache_kt.shape[-1]
    p_len = n_pages * page
    tri = (jnp.arange(page)[:, None] > jnp.arange(page)[None, :]).astype(BF16)
    hrow = pl.BlockSpec((None, n_heads, HEAD_DIM), lambda i, pt: (i, 0, 0))
    frow = pl.BlockSpec((None, 1, LANES), lambda i, pt: (i, 0, 0))
    const = lambda a: pl.BlockSpec(a.shape, lambda i, pt: (0,) * a.ndim)
    kern = functools.partial(_decode_kernel, layer=layer, n_pages=n_pages, page=page, n_heads=n_heads)
    buf = pltpu.VMEM((2, n_pages, n_heads, HEAD_DIM, page), F32)
    grid_spec = pltpu.PrefetchScalarGridSpec(
        num_scalar_prefetch=1, grid=(bs,),
        in_specs=[hrow, hrow, hrow, frow, const(b_pad),
                  pl.BlockSpec((None, n_heads, p_len), lambda i, pt: (i, 0, 0)), const(tri),
                  pl.BlockSpec(memory_space=pl.ANY), pl.BlockSpec(memory_space=pl.ANY)],
        out_specs=[hrow, frow],
        scratch_shapes=[buf, buf, pltpu.SemaphoreType.DMA((2,)), pltpu.SemaphoreType.DMA((2,))])
    o, lfn = pl.pallas_call(
        kern, grid_spec=grid_spec, name="decode_attention",
        out_shape=[jax.ShapeDtypeStruct((bs, n_heads, HEAD_DIM), F32), jax.ShapeDtypeStruct((bs, 1, LANES), F32)],
        compiler_params=_params("arbitrary"),
    )(page_table.reshape(-1), q, k_new, v_new, f_raw.reshape(bs, 1, LANES), b_pad, lfp_t, tri,
      cache_kt, cache_vt)
    return o, lfn.reshape(bs, LANES)


def _outproj_kernel(x_ref, o_ref, y_ref, ga_ref, gb_ref, woa_ref, wob_ref, gf_ref, wr_hi_ref, wr_lo_ref,
                    br_ref, x2_ref, xn2_ref, route_ref, *, att_padded, d_att, c):
    tm = x_ref.shape[0]
    if att_padded:
        o = o_ref[...]
        lane3 = lax.broadcasted_iota(jnp.int32, o.shape, 2)
        o = jnp.where(lane3 < HEAD_DIM, o, 0.0)
        ssq = jnp.sum(jnp.sum(o * o, axis=0), axis=-1, keepdims=True)
        inv = lax.rsqrt(ssq / d_att + EPS)
        y = jnp.zeros((tm, x_ref.shape[1]), F32)
        for h in range(o.shape[0]):
            y = y + _dot((o[h] * inv * ga_ref[h]).astype(BF16), woa_ref[h])
    else:
        y = _dot(_rms(o_ref[...], ga_ref[...]).astype(BF16), woa_ref[...])
    y = y + _dot(_rms(y_ref[:, 0:c], gb_ref[:, 0:c]).astype(BF16), wob_ref[0:c, :])
    y = y + _dot(_rms(y_ref[:, c:2 * c], gb_ref[:, c:2 * c]).astype(BF16), wob_ref[c:2 * c, :])
    x2 = x_ref[...] + y
    x2_ref[...] = x2
    xn2 = _rms(x2, gf_ref[...])
    n_chunks = xn2.shape[1] // LANES
    for ch in range(n_chunks):
        xn2_ref[pl.ds(ch, tm, stride=n_chunks), :] = xn2[:, ch * LANES:(ch + 1) * LANES]

    hi = xn2.astype(BF16)
    lo = (xn2 - hi.astype(F32)).astype(BF16)
    logit = _dot(hi, wr_hi_ref[...]) + _dot(lo, wr_hi_ref[...]) + _dot(hi, wr_lo_ref[...]) + br_ref[...]
    lane = lax.broadcasted_iota(jnp.int32, (tm, LANES), 1).astype(F32)
    far = float(LANES)

    def first_argmax(vals, valid):
        mx = jnp.max(jnp.where(valid, vals, NEG), axis=-1, keepdims=True)
        idx = jnp.min(jnp.where(valid & (vals == mx), lane, far), axis=-1, keepdims=True)
        return mx, idx

    gmask = lane < N_GROUPS
    gmax, gidx = first_argmax(logit, gmask)
    pg_top = 1.0 / jnp.sum(jnp.where(gmask, jnp.exp(logit - gmax), 0.0), axis=-1, keepdims=True)
    e_lo = N_GROUPS + EXPERTS_PER_GROUP * gidx
    emask = (lane >= e_lo) & (lane < e_lo + EXPERTS_PER_GROUP)
    m1, i1 = first_argmax(logit, emask)
    m2, i2 = first_argmax(logit, emask & (lane != i1))
    e = jnp.exp(m2 - m1)
    g1 = pg_top / (1.0 + e)
    g2 = pg_top * e / (1.0 + e)
    route_ref[...] = jnp.where(lane == 0, i1 - N_GROUPS, jnp.where(lane == 1, i2 - N_GROUPS,
                               jnp.where(lane == 2, g1, jnp.where(lane == 3, g2, 0.0))))


def _outproj(x, o, y, ow, *, tm, att_padded):
    b, t, d = x.shape
    c = y.shape[-1] // 2
    d_att = 512
    row = lambda n: pl.BlockSpec((None, tm, n), lambda i, j: (i, j, 0))
    const = lambda a: pl.BlockSpec(a.shape, lambda i, j: (0,) * a.ndim)
    if att_padded:
        o_spec = pl.BlockSpec((None, o.shape[1], tm, LANES), lambda i, j: (i, 0, j, 0))
    else:
        o_spec = row(d_att)
    kern = functools.partial(_outproj_kernel, att_padded=att_padded, d_att=d_att, c=c)
    return pl.pallas_call(
        kern, grid=(b, t // tm), name="outproj_heads" if att_padded else "outproj_flat",
        in_specs=[row(d), o_spec, row(2 * c)] + [const(a) for a in ow],
        out_specs=[row(d), pl.BlockSpec((None, tm * d // LANES, LANES), lambda i, j: (i, j, 0)), row(LANES)],
        out_shape=[jax.ShapeDtypeStruct((b, t, d), F32), jax.ShapeDtypeStruct((b, t * d // LANES, LANES), F32),
                   jax.ShapeDtypeStruct((b, t, LANES), F32)],
        compiler_params=_params("parallel", "parallel"),
    )(x, o, y, *ow)


def _rank_kernel(route_ref, cin_ref, tri_ref, rank_ref, cnt_ref, carry_ref):
    @pl.when(pl.program_id(0) == 0)
    def _():
        carry_ref[...] = cin_ref[...]

    route = route_ref[...]
    tm = route.shape[0]
    lane = lax.broadcasted_iota(jnp.int32, (tm, LANES), 1).astype(F32)
    is1 = lane == route[:, 0:1]
    is2 = lane == route[:, 1:2]
    onehot = jnp.where(is1, 1.0, jnp.where(is2, 1.0, 0.0))
    before = _dot(tri_ref[...], onehot.astype(BF16)) + carry_ref[...]
    r1 = jnp.sum(jnp.where(is1, before, 0.0), axis=-1, keepdims=True)
    r2 = jnp.sum(jnp.where(is2, before, 0.0), axis=-1, keepdims=True)
    rank_ref[...] = jnp.where(lane == 0, r1, jnp.where(lane == 1, r2, 0.0))
    carry_ref[...] = carry_ref[...] + jnp.sum(onehot, axis=0, keepdims=True)
    cnt_ref[...] = carry_ref[...]


def _rank(route, counts_in, *, tm):
    n = route.shape[0]
    tri = (jnp.arange(tm)[:, None] > jnp.arange(tm)[None, :]).astype(BF16)
    return pl.pallas_call(
        _rank_kernel, grid=(n // tm,), name="rank",
        in_specs=[pl.BlockSpec((tm, LANES), lambda i: (i, 0)), pl.BlockSpec((1, LANES), lambda i: (0, 0)),
                  pl.BlockSpec((tm, tm), lambda i: (0, 0))],
        out_specs=[pl.BlockSpec((tm, LANES), lambda i: (i, 0)), pl.BlockSpec((1, LANES), lambda i: (0, 0))],
        out_shape=[jax.ShapeDtypeStruct((n, LANES), F32), jax.ShapeDtypeStruct((1, LANES), F32)],
        scratch_shapes=[pltpu.VMEM((1, LANES), F32)],
        compiler_params=_params("arbitrary"),
    )(route, counts_in, tri)


def _dispatch(e_all, rank_all, counts, n_blocks):
    blk = MOE_BLOCK
    nk = e_all.shape[0] * TOP_K
    nblk_e = (counts + blk - 1) // blk
    bend = jnp.cumsum(nblk_e)
    bstart = bend - nblk_e
    onehot = e_all[:, :, None] == jnp.arange(N_EXPERTS, dtype=jnp.int32)[None, None, :]
    dest = rank_all + jnp.sum(jnp.where(onehot, (bstart * blk)[None, None, :], 0), axis=-1)
    dest = dest.reshape(nk).astype(jnp.int32)
    n_used = bend[-1].astype(jnp.int32)
    bidx = jnp.arange(n_blocks, dtype=jnp.int32)
    blk_e = jnp.minimum(jnp.sum((bend[None, :] <= bidx[:, None]).astype(jnp.int32), axis=1), N_EXPERTS - 1)
    eq = blk_e[:, None] == jnp.arange(N_EXPERTS, dtype=jnp.int32)[None, :]
    cnt_b = jnp.sum(jnp.where(eq, counts[None, :], 0), axis=1)
    start_b = jnp.sum(jnp.where(eq, bstart[None, :], 0), axis=1)
    n_valid = jnp.clip(cnt_b - (bidx - start_b) * blk, 0, blk)
    n_valid = jnp.where(bidx < n_used, n_valid, 0).astype(jnp.int32)
    last_e = jnp.sum(jnp.where(bidx == n_used - 1, blk_e, 0))
    blk_e = jnp.where(bidx < n_used, blk_e, last_e).astype(jnp.int32)
    return blk_e, n_valid, n_used.reshape(1), dest


def _for_rows(n, fn):
    n_main = n // DMA_UNROLL

    def main(t, _):
        for u in range(DMA_UNROLL):
            fn(t * DMA_UNROLL + u)
        return 0

    def rest(r, _):
        fn(r)
        return 0

    lax.fori_loop(0, n_main, main, 0)
    lax.fori_loop(n_main * DMA_UNROLL, n, rest, 0)


def _moe_kernel(be_ref, nv_ref, nu_ref, dest_ref, xn_hbm, wg_ref, wu_ref, wd_ref, y_hbm,
                xbuf, ybuf, gsem, ssem, wgb, wub, wdb, src_ref, *, blk, n_chunks):
    i = pl.program_id(0)
    n_used = nu_ref[0]
    slot = i % 2

    @pl.when(i == 0)
    def _():
        def put(a):
            src_ref[dest_ref[a]] = a
        _for_rows(dest_ref.shape[0], put)

    def rows(t):
        return pl.ds(pl.multiple_of(t * n_chunks, n_chunks), n_chunks)

    def gather_copy(bi, sl, r):
        tok = lax.shift_right_logical(src_ref[bi * blk + r], TOP_K.bit_length() - 1)
        return pltpu.make_async_copy(xn_hbm.at[rows(tok)], xbuf.at[sl, rows(r)], gsem.at[sl])

    def scatter_copy(bi, sl, r):
        return pltpu.make_async_copy(ybuf.at[sl, rows(r)], y_hbm.at[rows(src_ref[bi * blk + r])], ssem.at[sl])

    def start_all(make, bi, sl):
        _for_rows(nv_ref[bi], lambda r: make(bi, sl, r).start())

    def wait_all(make, bi, sl):
        _for_rows(nv_ref[bi], lambda r: make(bi, sl, 0).wait())

    @pl.when(i == 0)
    def _():
        xbuf[...] = jnp.zeros_like(xbuf)

        @pl.when(n_used > 0)
        def _():
            start_all(gather_copy, 0, 0)

    @pl.when(i + 1 < n_used)
    def _():
        start_all(gather_copy, i + 1, 1 - slot)

    @pl.when(i < n_used)
    def _():
        @pl.when((i == 0) | (be_ref[i] != be_ref[jnp.maximum(i - 1, 0)]))
        def _():
            wgb[...] = wg_ref[...].astype(BF16)
            wub[...] = wu_ref[...].astype(BF16)
            wdb[...] = wd_ref[...].astype(BF16)

        wait_all(gather_copy, i, slot)
        x = jnp.concatenate([xbuf[slot, pl.ds(ch, blk, stride=n_chunks), :] for ch in range(n_chunks)],
                            axis=1).astype(BF16)
        g = _dot(x, wgb[...])
        u = _dot(x, wub[...])
        hmid = (g * _sigmoid(g) * u).astype(BF16)
        y = _dot(hmid, wdb[...])

        @pl.when(i >= 2)
        def _():
            wait_all(scatter_copy, i - 2, slot)

        for ch in range(n_chunks):
            ybuf[slot, pl.ds(ch, blk, stride=n_chunks), :] = y[:, ch * LANES:(ch + 1) * LANES]
        start_all(scatter_copy, i, slot)

        @pl.when(i == n_used - 1)
        def _():
            @pl.when(i >= 1)
            def _():
                wait_all(scatter_copy, i - 1, 1 - slot)

            wait_all(scatter_copy, i, slot)


def _moe_experts(blk_e, n_valid, n_used, dest, xn_all, w_gate, w_up, w_down, *, layer, n_blocks):
    d = w_gate.shape[-2]
    n_chunks = d // LANES
    n_all = xn_all.shape[0] // n_chunks
    de = w_gate.shape[-1]
    blk = MOE_BLOCK
    wspec = lambda r, c_: pl.BlockSpec((None, None, r, c_), lambda i, be, nv, nu, s: (layer, be[i], 0, 0))
    grid_spec = pltpu.PrefetchScalarGridSpec(
        num_scalar_prefetch=4, grid=(n_blocks,),
        in_specs=[pl.BlockSpec(memory_space=pl.ANY), wspec(d, de), wspec(d, de), wspec(de, d)],
        out_specs=pl.BlockSpec(memory_space=pl.ANY),
        scratch_shapes=[pltpu.VMEM((2, blk * n_chunks, LANES), F32), pltpu.VMEM((2, blk * n_chunks, LANES), F32),
                        pltpu.SemaphoreType.DMA((2,)), pltpu.SemaphoreType.DMA((2,)),
                        pltpu.VMEM((d, de), BF16), pltpu.VMEM((d, de), BF16), pltpu.VMEM((de, d), BF16),
                        pltpu.SMEM((n_blocks * blk,), jnp.int32)])
    return pl.pallas_call(
        functools.partial(_moe_kernel, blk=blk, n_chunks=n_chunks), grid_spec=grid_spec, name="moe_experts",
        out_shape=jax.ShapeDtypeStruct((n_all * TOP_K * n_chunks, LANES), F32),
        compiler_params=_params("arbitrary"),
    )(blk_e, n_valid, n_used, dest, xn_all, w_gate, w_up, w_down)


def _final_kernel(x2_ref, y_ref, route_ref, g_ref, o_ref):
    o_ref[...] = _rms(_moe_combine(x2_ref[...], route_ref[...], y_ref), g_ref[...])


def _final(x2, y, route, g, *, tm, y_block_offset):
    n, d = x2.shape
    return pl.pallas_call(
        _final_kernel, grid=(n // tm,), name="final",
        in_specs=[pl.BlockSpec((tm, d), lambda i: (i, 0)),
                  pl.BlockSpec((tm * TOP_K * d // LANES, LANES), lambda i: (y_block_offset + i, 0)),
                  pl.BlockSpec((tm, LANES), lambda i: (i, 0)), pl.BlockSpec((1, d), lambda i: (0, 0))],
        out_specs=pl.BlockSpec((tm, d), lambda i: (i, 0)),
        out_shape=jax.ShapeDtypeStruct((n, d), F32),
        compiler_params=_params("parallel"),
    )(x2, y, route, g)


def _block_diag(w):
    n, a, b = w.shape
    out = jnp.zeros((n * a, n * b), w.dtype)
    for i in range(n):
        out = out.at[i * a:(i + 1) * a, i * b:(i + 1) * b].set(w[i])
    return out


def _pad_lanes(v, n=LANES):
    return jnp.zeros((1, n), F32).at[0, :v.shape[0]].set(v)


def _tile_for(t, pref):
    tm = min(pref, t)
    while t % tm:
        tm //= 2
    return tm


def kernel(x_prompt, x_sample, cache_k, cache_v, cache_logf, state_conv, state_rglru, state_pool, page_table, norm_mix, w_in, b_forget, conv_w, conv_b, w_gate_a, b_gate_a, w_gate_x, b_gate_x, lru_lambda, w_pool, pool_scale, norm_groups, w_out, norm_ffn, w_router_group, b_router_group, w_router_expert, b_router_expert, w_up, w_gate, w_down, norm_final):
    depth = w_in.shape[0]
    b_p, t_p, d = x_prompt.shape
    b_s = x_sample.shape[0]
    n_heads = b_forget.shape[1]
    d_att = n_heads * HEAD_DIM
    c = conv_w.shape[-1]
    n_p = b_p * t_p
    n_all = n_p + b_s
    n_pages, page = page_table.shape[1], cache_k.shape[2]
    past_len = n_pages * page
    nk = n_all * TOP_K
    n_blocks = -(-(nk + N_EXPERTS * (MOE_BLOCK - 1)) // MOE_BLOCK)

    tm_p = _tile_for(t_p, 512)
    tk = _tile_for(t_p, 512)
    n_chains = 2 if t_p % (2 * tk) == 0 else 1
    tr = _tile_for(t_p, 256)
    tm_r = _tile_for(n_p, 1024)
    tm_f = _tile_for(n_p, 512)

    cache_kt = cache_k.transpose(0, 1, 3, 4, 2)
    cache_vt = cache_v.transpose(0, 1, 3, 4, 2)
    x_p = x_prompt
    x_s = x_sample.reshape(1, b_s, d)
    moe_p = moe_s = None
    off_f = 3 * d_att
    outs = [[] for _ in range(12)]
    for l in range(depth):
        wl = w_in[l]
        w_cat = jnp.concatenate(
            [wl[:, :off_f], wl[:, off_f + n_heads:], wl[:, off_f:off_f + n_heads],
             jnp.zeros((d, LANES - n_heads), F32)], axis=1).astype(BF16)
        g_mix = norm_mix[l].reshape(1, d)
        b_pad = _pad_lanes(b_forget[l])
        mw = (conv_w[l], conv_b[l].reshape(1, c), _block_diag(w_gate_a[l]).astype(BF16),
              b_gate_a[l].reshape(1, c), _block_diag(w_gate_x[l]).astype(BF16), b_gate_x[l].reshape(1, c),
              lru_lambda[l].reshape(1, c), _block_diag(w_pool[l]).astype(BF16), pool_scale[l].reshape(1, c))
        ng, wo = norm_groups[l], w_out[l]
        ga_pad = jnp.zeros((n_heads, 1, LANES), F32).at[:, 0, :HEAD_DIM].set(ng[:d_att].reshape(n_heads, HEAD_DIM))
        woa_pad = jnp.zeros((n_heads, LANES, d), F32).at[:, :HEAD_DIM].set(
            wo[:d_att].reshape(n_heads, HEAD_DIM, d)).astype(BF16)
        w_r = jnp.zeros((d, LANES), F32).at[:, :N_GROUPS].set(w_router_group[l]).at[
            :, N_GROUPS:N_GROUPS + N_EXPERTS].set(w_router_expert[l])
        wr_hi = w_r.astype(BF16)
        wr_lo = (w_r - wr_hi.astype(F32)).astype(BF16)
        b_r = _pad_lanes(jnp.concatenate([b_router_group[l], b_router_expert[l]]))
        ow_rest = (wo[d_att:].astype(BF16), norm_ffn[l].reshape(1, d), wr_hi, wr_lo, b_r)
        g_rest = ng[d_att:].reshape(1, 2 * c)
        ow_p = (ga_pad, g_rest, woa_pad, *ow_rest)
        ow_s = (ng[:d_att].reshape(1, d_att), g_rest, wo[:d_att].astype(BF16), *ow_rest)

        res = _inproj(x_p, g_mix, w_cat, tm=tm_p, head_layout=True, moe=moe_p, b_pad=b_pad)
        if moe_p is not None:
            x_p, res = res[0], res[1:]
        qa, ka, vp, k_p, v_p, lf_p, rest_p = res
        o_p = _attention(qa, ka, vp, tk=tk, n_chains=n_chains)
        y_mix_p, conv_t, h_t, pool_t = _mix_prompt(rest_p, mw, tr=tr)
        x2_p, xn2_p, route_p = _outproj(x_p, o_p, y_mix_p, ow_p, tm=tm_p, att_padded=True)

        res = _inproj(x_s, g_mix, w_cat, tm=b_s, head_layout=False, moe=moe_s, y_block_offset=n_p // b_s)
        if moe_s is not None:
            x_s, res = res[0], res[1:]
        q_s, k_s, v_s, f_s, rest_s = res
        heads = lambda a: a.reshape(b_s, n_heads, HEAD_DIM)
        lfp_t = cache_logf[l][page_table].reshape(b_s, past_len, n_heads).transpose(0, 2, 1)
        o_s, lf_s = _decode_attention(page_table, heads(q_s), heads(k_s), heads(v_s), f_s[0], b_pad, lfp_t,
                                      cache_kt, cache_vt, layer=l)
        y_mix_s, conv_s, h_s, pool_s = _mix_sample(
            rest_s[0], state_conv[l].transpose(1, 0, 2), state_rglru[l], state_pool[l].transpose(1, 0, 2),
            mw, pos0=past_len)
        x2_s, xn2_s, route_s = _outproj(x_s, o_s.reshape(1, b_s, d_att), y_mix_s[None], ow_s, tm=b_s,
                                        att_padded=False)

        route_p2, route_s2 = route_p.reshape(n_p, LANES), route_s.reshape(b_s, LANES)
        rank_p, cnt_p = _rank(route_p2, jnp.zeros((1, LANES), F32), tm=tm_r)
        rank_s, cnt = _rank(route_s2, cnt_p, tm=b_s)
        first2 = lambda a_p, a_s: jnp.concatenate([a_p[:, :TOP_K], a_s[:, :TOP_K]], axis=0).astype(jnp.int32)
        blk_e, n_valid, n_used, dest = _dispatch(first2(route_p2, route_s2), first2(rank_p, rank_s),
                                                cnt[0, :N_EXPERTS].astype(jnp.int32), n_blocks)
        xn2_all = jnp.concatenate([xn2_p.reshape(-1, LANES), xn2_s.reshape(-1, LANES)], axis=0)
        y_moe = _moe_experts(blk_e, n_valid, n_used, dest, xn2_all, w_gate, w_up, w_down,
                             layer=l, n_blocks=n_blocks)
        x_p, x_s = x2_p, x2_s
        moe_p, moe_s = (y_moe, route_p), (y_moe, route_s)

        per_layer = (
            k_p.reshape(b_p, n_heads, HEAD_DIM, t_p).transpose(0, 3, 1, 2),
            v_p.reshape(b_p, n_heads, HEAD_DIM, t_p).transpose(0, 3, 1, 2),
            lf_p[:, :, :n_heads], conv_t[:, 8 - (CONV_W - 1):], h_t[:, 7], pool_t[:, 16 - POOL_BUF:],
            k_s.reshape(b_s, 1, n_heads, HEAD_DIM), v_s.reshape(b_s, 1, n_heads, HEAD_DIM),
            lf_s[:, None, :n_heads], conv_s.transpose(1, 0, 2), h_s, pool_s.transpose(1, 0, 2))
        for acc, val in zip(outs, per_layer):
            acc.append(val)
    g_fin = norm_final.reshape(1, d)
    y_prompt = _final(x_p.reshape(n_p, d), moe_p[0], moe_p[1].reshape(n_p, LANES), g_fin,
                      tm=tm_f, y_block_offset=0).reshape(b_p, t_p, d)
    y_sample = _final(x_s.reshape(b_s, d), moe_s[0], moe_s[1].reshape(b_s, LANES), g_fin,
                      tm=b_s, y_block_offset=n_p // b_s).reshape(b_s, 1, d)
    return (y_prompt, y_sample, *[jnp.stack(o) for o in outs])
```
